```python
import math
import jax
import jax.numpy as jnp
from jax import lax
import numpy as np

D_MODEL = 4096
BATCH = 4
SEQ = 2048
DEPTH = 4
DEC_BATCH = 8
DEC_SEQ = 1
PAST_LEN = 8192
PAGE_SIZE = 128

HEAD_DIM = 128
MIX_WIDTH = D_MODEL
GROUP_WIDTH = MIX_WIDTH // 4
S5_WIDTH = GROUP_WIDTH
S5_CH = 16
S5_GROUPS = S5_WIDTH // S5_CH
S5_STATE = 64
SB_HEADS = GROUP_WIDTH // HEAD_DIM
SB_Q_BLOCK = 128
MOBA_HEADS = GROUP_WIDTH // HEAD_DIM
MOBA_BLOCK = 256
MOBA_TOPK = 3
MOBA_Q_BLOCK = 64
DN_HEADS = GROUP_WIDTH // HEAD_DIM
DN_DK = HEAD_DIM
DN_DV = HEAD_DIM
DN_CONV = 4
DN_CHUNK = 64
D_FF = 11008
FFN_CONV = 3
N_MOD = 6
EPS = 1e-6
IN_SIZES = (S5_WIDTH, 3 * SB_HEADS * HEAD_DIM, 3 * MOBA_HEADS * HEAD_DIM,
            DN_HEADS * (2 * DN_DK + DN_DV), DN_HEADS * DN_DV, DN_HEADS, DN_HEADS)
IN_WIDTH = sum(IN_SIZES)
IN_SPLITS = tuple(int(s) for s in np.cumsum(IN_SIZES)[:-1])
F32 = jnp.float32

kernel_name = 'hybrid_parallel_group_decoder_step'


def rms_norm(x, g):
    xf = x.astype(F32)
    y = xf * lax.rsqrt(jnp.mean(xf * xf, axis=-1, keepdims=True) + EPS)
    return (y * g.astype(F32)).astype(x.dtype)


def l2_norm(x):
    return x * lax.rsqrt(jnp.sum(x * x, axis=-1, keepdims=True) + EPS)


def causal_dwconv(x, prev, w, b):
    K = w.shape[0]
    T = x.shape[1]
    xp = jnp.concatenate([prev.astype(x.dtype), x], axis=1)
    y = b + xp[:, 0:T] * w[0]
    for i in range(1, K):
        y = y + xp[:, i:i + T] * w[i]
    return y, xp[:, T:]


def alibi_slopes(n):
    return 2.0 ** (-8.0 * (jnp.arange(n, dtype=F32) + 1.0) / n)


def map_query_blocks(fn, q, q_pos, blk):
    B, Q, H, dh = q.shape
    bq = min(blk, Q)
    n = -(-Q // bq)
    pad = n * bq - Q
    qp = jnp.pad(q, ((0, 0), (0, pad), (0, 0), (0, 0)))
    pp = jnp.pad(q_pos, (0, pad), mode='edge')
    qb = qp.reshape(B, n, bq, H, dh).swapaxes(0, 1)
    pb = pp.reshape(n, bq)
    out = lax.map(lambda args: fn(args[0], args[1]), (qb, pb))
    return out.swapaxes(0, 1).reshape(B, n * bq, H, -1)[:, :Q]


def _complex_affine_combine(e1, e2):
    a1r, a1i, b1r, b1i = e1
    a2r, a2i, b2r, b2i = e2
    return (a2r * a1r - a2i * a1i, a2r * a1i + a2i * a1r,
            a2r * b1r - a2i * b1i + b2r, a2r * b1i + a2i * b1r + b2i)


def s5_mixer(u, h0, lam_re, lam_im, log_dt, b_re, b_im, c_re, c_im, d_skip, w_glu, b_glu):
    B, T, _ = u.shape
    uf = u.astype(F32).reshape(B, T, S5_GROUPS, S5_CH)
    lr, li = lam_re.astype(F32), lam_im.astype(F32)
    dt = jnp.exp(log_dt.astype(F32))[:, None]
    mag = jnp.exp(lr * dt)
    ar, ai = mag * jnp.cos(li * dt), mag * jnp.sin(li * dt)
    den = lr * lr + li * li
    fr = ((ar - 1.0) * lr + ai * li) / den
    fi = (ai * lr - (ar - 1.0) * li) / den
    br, bi = b_re.astype(F32), b_im.astype(F32)
    bbr = fr[..., None] * br - fi[..., None] * bi
    bbi = fr[..., None] * bi + fi[..., None] * br
    bu_r = jnp.einsum('btgc,gpc->btgp', uf, bbr)
    bu_i = jnp.einsum('btgc,gpc->btgp', uf, bbi)
    a_r = jnp.broadcast_to(ar, bu_r.shape)
    a_i = jnp.broadcast_to(ai, bu_r.shape)
    pr, pi_, hr, hi = lax.associative_scan(_complex_affine_combine, (a_r, a_i, bu_r, bu_i), axis=1)
    h0r = h0[..., 0].astype(F32)[:, None]
    h0i = h0[..., 1].astype(F32)[:, None]
    hr = hr + pr * h0r - pi_ * h0i
    hi = hi + pr * h0i + pi_ * h0r
    y = (jnp.einsum('btgp,gcp->btgc', hr, c_re.astype(F32))
         - jnp.einsum('btgp,gcp->btgc', hi, c_im.astype(F32)))
    y = y + uf * d_skip.astype(F32).reshape(S5_GROUPS, S5_CH)
    y = jax.nn.gelu(y.reshape(B, T, S5_WIDTH))
    y = y * jax.nn.sigmoid(y @ w_glu.astype(F32) + b_glu.astype(F32))
    h_last = jnp.stack([hr[:, -1], hi[:, -1]], axis=-1)
    return y, h_last


def stick_breaking_block(q, q_pos, k, v, k_pos):
    z = jnp.einsum('bqhd,bkhd->bhqk', q.astype(F32), k.astype(F32)) * (HEAD_DIM ** -0.5)
    mask = (k_pos[None, :] < q_pos[:, None])[None, None]
    log_keep = jnp.where(mask, jax.nn.log_sigmoid(-z), 0.0)
    log_after = lax.cumsum(log_keep, axis=3, reverse=True) - log_keep
    w = jnp.where(mask, jnp.exp(jax.nn.log_sigmoid(z) + log_after), 0.0)
    return jnp.einsum('bhqk,bkhd->bqhd', w, v.astype(F32))


def moba_blocks(k, v):
    B, L, H, dh = k.shape
    nb = -(-L // MOBA_BLOCK)
    pad = nb * MOBA_BLOCK - L
    cfg = ((0, 0), (0, pad), (0, 0), (0, 0))
    kb = jnp.pad(k, cfg).reshape(B, nb, MOBA_BLOCK, H, dh).transpose(0, 3, 1, 2, 4)
    vb = jnp.pad(v, cfg).reshape(B, nb, MOBA_BLOCK, H, dh).transpose(0, 3, 1, 2, 4)
    kmean = jnp.mean(kb.astype(F32), axis=3)
    return kb, vb, kmean


def moba_block(q, q_pos, kb, vb, kmean, slopes):
    B, H, NB, BS, dh = kb.shape
    Q = q.shape[1]
    qf = q.astype(F32).transpose(0, 2, 1, 3)
    n_past = (q_pos // MOBA_BLOCK).astype(jnp.int32)
    gate = jnp.einsum('bhqd,bhnd->bhqn', qf, kmean)
    gate = jnp.where((jnp.arange(NB)[None, :] < n_past[:, None])[None, None], gate, -jnp.inf)
    k_sel = min(MOBA_TOPK, NB)
    _, top_idx = lax.top_k(gate, k_sel)
    own = jnp.broadcast_to(n_past[None, None, :, None], (B, H, Q, 1))
    sel = jnp.concatenate([top_idx.astype(jnp.int32), own], axis=-1)
    sel_ok = jnp.concatenate([jnp.arange(k_sel)[None, :] < n_past[:, None],
                              jnp.ones((Q, 1), dtype=bool)], axis=-1)
    bi = jnp.arange(B)[:, None, None]
    hi = jnp.arange(H)[None, :, None]
    offs = jnp.arange(BS)
    logits = []
    for j in range(k_sel + 1):
        idx = sel[..., j]
        kj = kb[bi, hi, idx].astype(F32)
        s = jnp.einsum('bhqd,bhqsd->bhqs', qf, kj) * (HEAD_DIM ** -0.5)
        dist = (q_pos[None, None, :, None] - (idx[..., None] * BS + offs)).astype(F32)
        ok = sel_ok[None, None, :, j, None] & (dist >= 0.0)
        logits.append(jnp.where(ok, s - slopes[None, :, None, None] * dist, -jnp.inf))
    p = jax.nn.softmax(jnp.concatenate(logits, axis=-1), axis=-1)
    out = jnp.zeros((B, H, Q, dh), F32)
    for j in range(k_sel + 1):
        vj = vb[bi, hi, sel[..., j]].astype(F32)
        out = out + jnp.einsum('bhqs,bhqsd->bhqd', p[..., j * BS:(j + 1) * BS], vj)
    return out.transpose(0, 2, 1, 3)


def gated_delta_rule(q, k, v, beta, g, S0):
    B, T, H, dk = q.shape
    dv = v.shape[-1]
    C = min(DN_CHUNK, T)
    n = -(-T // C)
    pad = n * C - T

    def chunks(a):
        a = jnp.pad(a.astype(F32), ((0, 0), (0, pad)) + ((0, 0),) * (a.ndim - 2))
        a = a.reshape((B, n, C) + a.shape[2:])
        return jnp.moveaxis(a, 3, 1)

    qc, kc, vc, bc, gc = chunks(q), chunks(k), chunks(v), chunks(beta), chunks(g)
    decay = jnp.cumsum(gc, axis=-1)
    tri = jnp.tril(jnp.ones((C, C), dtype=bool))
    strict = jnp.tril(jnp.ones((C, C), dtype=bool), -1)
    Lmat = jnp.exp(jnp.where(tri, decay[..., :, None] - decay[..., None, :], -jnp.inf))
    kbeta = kc * bc[..., None]
    A = jnp.where(strict, jnp.einsum('bhnid,bhnjd->bhnij', kbeta, kc) * Lmat, 0.0)
    M = A + jnp.eye(C, dtype=F32)
    u = lax.linalg.triangular_solve(M, vc * bc[..., None], left_side=True, lower=True, unit_diagonal=True)
    w = lax.linalg.triangular_solve(M, kbeta * jnp.exp(decay)[..., None], left_side=True, lower=True, unit_diagonal=True)
    qk = jnp.einsum('bhnid,bhnjd->bhnij', qc, kc) * Lmat

    def step(S, xs):
        q_c, k_c, u_c, w_c, qk_c, d_c = xs
        v_new = u_c - jnp.einsum('bhcd,bhde->bhce', w_c, S)
        o = (jnp.einsum('bhcd,bhde->bhce', q_c * jnp.exp(d_c)[..., None], S)
             + jnp.einsum('bhij,bhje->bhie', qk_c, v_new))
        d_last = d_c[..., -1:]
        S = (S * jnp.exp(d_last)[..., None]
             + jnp.einsum('bhcd,bhce->bhde', k_c * jnp.exp(d_last - d_c)[..., None], v_new))
        return S, o

    xs = tuple(jnp.moveaxis(a, 2, 0) for a in (qc, kc, u, w, qk, decay))
    S, o = lax.scan(step, S0.astype(F32), xs)
    o = jnp.moveaxis(jnp.moveaxis(o, 0, 2), 1, 3).reshape(B, n * C, H, dv)[:, :T]
    return o, S


def deltanet_mixer(qkv, gate, beta_in, a_in, conv_prev, S0, conv_w, conv_b, a_log, dt_bias, norm_g):
    B, T, _ = qkv.shape
    qkv_c, conv_new = causal_dwconv(qkv, conv_prev, conv_w, conv_b)
    qkv_c = jax.nn.silu(qkv_c.astype(F32))
    q, k, v = jnp.split(qkv_c, [DN_HEADS * DN_DK, 2 * DN_HEADS * DN_DK], axis=-1)
    q = l2_norm(q.reshape(B, T, DN_HEADS, DN_DK)) * (DN_DK ** -0.5)
    k = l2_norm(k.reshape(B, T, DN_HEADS, DN_DK))
    v = v.reshape(B, T, DN_HEADS, DN_DV)
    beta = jax.nn.sigmoid(beta_in.astype(F32))
    g = -jnp.exp(a_log.astype(F32)) * jax.nn.softplus(a_in.astype(F32) + dt_bias.astype(F32))
    o, S = gated_delta_rule(q, k, v, beta, g, S0)
    o = rms_norm(o, norm_g) * jax.nn.silu(gate.astype(F32).reshape(B, T, DN_HEADS, DN_DV))
    return o.reshape(B, T, GROUP_WIDTH), S, conv_new


def mixing_sublayer(h, q_pos, l, W, s5_h0, dn_S0, dn_conv_prev, sb_past, moba_past):
    B, T, _ = h.shape
    proj = h @ W['w_in'][l]
    u_s5, qkv_sb, qkv_mb, qkv_dn, gate_dn, beta_dn, a_dn = jnp.split(proj, IN_SPLITS, axis=-1)
    GW = GROUP_WIDTH
    y_s5, s5_new = s5_mixer(u_s5, s5_h0, W['s5_lambda_re'][l], W['s5_lambda_im'][l], W['s5_log_dt'][l],
                            W['s5_b_re'][l], W['s5_b_im'][l], W['s5_c_re'][l], W['s5_c_im'][l],
                            W['s5_d'][l], W['s5_w_glu'][l], W['s5_b_glu'][l])
    q_sb = qkv_sb[..., :GW].reshape(B, T, SB_HEADS, HEAD_DIM)
    kv_sb = qkv_sb[..., GW:].reshape(B, T, 2, SB_HEADS, HEAD_DIM)
    kv_all = kv_sb if sb_past is None else jnp.concatenate([sb_past.astype(kv_sb.dtype), kv_sb], axis=1)
    k_pos = jnp.arange(kv_all.shape[1], dtype=jnp.int32)
    k_sb, v_sb = kv_all[:, :, 0], kv_all[:, :, 1]
    y_sb = map_query_blocks(lambda qb, pb: stick_breaking_block(qb, pb, k_sb, v_sb, k_pos),
                            q_sb, q_pos, SB_Q_BLOCK)
    q_mb = qkv_mb[..., :GW].reshape(B, T, MOBA_HEADS, HEAD_DIM)
    kv_mb = qkv_mb[..., GW:].reshape(B, T, 2, MOBA_HEADS, HEAD_DIM)
    kvm_all = kv_mb if moba_past is None else jnp.concatenate([moba_past.astype(kv_mb.dtype), kv_mb], axis=1)
    kb, vb, kmean = moba_blocks(kvm_all[:, :, 0], kvm_all[:, :, 1])
    slopes = alibi_slopes(MOBA_HEADS)
    y_mb = map_query_blocks(lambda qb, pb: moba_block(qb, pb, kb, vb, kmean, slopes),
                            q_mb, q_pos, MOBA_Q_BLOCK)
    y_dn, dn_new, dn_conv_new = deltanet_mixer(qkv_dn, gate_dn, beta_dn, a_dn, dn_conv_prev, dn_S0,
                                               W['dn_conv_w'][l], W['dn_conv_b'][l], W['dn_a_log'][l],
                                               W['dn_dt_bias'][l], W['dn_norm_g'][l])
    y = jnp.concatenate([y_s5.astype(h.dtype), y_sb.reshape(B, T, GW).astype(h.dtype),
                         y_mb.reshape(B, T, GW).astype(h.dtype), y_dn.astype(h.dtype)], axis=-1)
    y = y @ W['w_out'][l]
    return y, (kv_sb, kv_mb, s5_new, dn_new, dn_conv_new)


def conv_ffn(h, prev, l, W):
    up = h @ W['ffn_w_up'][l]
    up_c, new_prev = causal_dwconv(up, prev, W['ffn_conv_w'][l], W['ffn_conv_b'][l])
    a, gt = jnp.split(up_c, 2, axis=-1)
    return (jax.nn.silu(gt) * a) @ W['ffn_w_down'][l], new_prev


def decoder_layer(x, c, q_pos, l, W, s5_h0, dn_S0, dn_conv_prev, ffn_prev, sb_past, moba_past):
    mod = (jax.nn.silu(c) @ W['w_ada'][l] + W['b_ada'][l]).reshape(c.shape[0], N_MOD, D_MODEL)
    shift_a, scale_a, gate_a, shift_f, scale_f, gate_f = [mod[:, i][:, None, :] for i in range(N_MOD)]
    h = rms_norm(x, W['g_pre_mix'][l]) * (1.0 + scale_a) + shift_a
    y, mix_state = mixing_sublayer(h, q_pos, l, W, s5_h0, dn_S0, dn_conv_prev, sb_past, moba_past)
    x = x + gate_a * rms_norm(y, W['g_post_mix'][l])
    h = rms_norm(x, W['g_pre_ffn'][l]) * (1.0 + scale_f) + shift_f
    y, ffn_state = conv_ffn(h, ffn_prev, l, W)
    x = x + gate_f * rms_norm(y, W['g_post_ffn'][l])
    return x, mix_state + (ffn_state,)


def stack_layers(states, i):
    return jnp.stack([s[i] for s in states], axis=0)


def setup_inputs(seed: int = 0) -> dict:
    key = jax.random.key(seed)
    keys = jax.random.split(key, 40)
    n_pages = PAST_LEN // PAGE_SIZE
    n_used = DEC_BATCH * n_pages
    n_phys = n_used + max(1, n_used // 4)

    def nrm(i, shape, scale=1.0):
        return scale * jax.random.normal(keys[i], shape, F32)

    G, P, CH = S5_GROUPS, S5_STATE, S5_CH
    page_table = jax.random.permutation(keys[0], n_phys)[:n_used].reshape(DEC_BATCH, n_pages).astype(jnp.int32)
    dn_dt = jnp.exp(jax.random.uniform(keys[32], (DEPTH, DN_HEADS), F32, math.log(1e-3), math.log(1e-1)))
    return {
        'x_prompt': nrm(1, (BATCH, SEQ, D_MODEL)),
        'x_sample': nrm(2, (DEC_BATCH, DEC_SEQ, D_MODEL)),
        'cache_sb_kv': nrm(3, (DEPTH, n_phys, PAGE_SIZE, 2, SB_HEADS, HEAD_DIM)),
        'cache_moba_kv': nrm(4, (DEPTH, n_phys, PAGE_SIZE, 2, MOBA_HEADS, HEAD_DIM)),
        'state_s5': nrm(5, (DEPTH, DEC_BATCH, G, P, 2), 0.5),
        'state_dn': nrm(6, (DEPTH, DEC_BATCH, DN_HEADS, DN_DK, DN_DV), 0.1),
        'state_dn_conv': nrm(7, (DEPTH, DEC_BATCH, DN_CONV - 1, 3 * GROUP_WIDTH)),
        'state_ffn_conv': nrm(8, (DEPTH, DEC_BATCH, FFN_CONV - 1, 2 * D_FF)),
        'page_table': page_table,
        'c_prompt': nrm(9, (BATCH, D_MODEL)),
        'c_sample': nrm(10, (DEC_BATCH, D_MODEL)),
        'w_in': nrm(11, (DEPTH, D_MODEL, IN_WIDTH), D_MODEL ** -0.5),
        'w_out': nrm(12, (DEPTH, MIX_WIDTH, D_MODEL), MIX_WIDTH ** -0.5),
        'w_ada': nrm(13, (DEPTH, D_MODEL, N_MOD * D_MODEL), 0.5 * D_MODEL ** -0.5),
        'b_ada': nrm(14, (DEPTH, N_MOD * D_MODEL), 0.01),
        'g_pre_mix': 1.0 + nrm(15, (DEPTH, D_MODEL), 0.01),
        'g_post_mix': 1.0 + nrm(16, (DEPTH, D_MODEL), 0.01),
        'g_pre_ffn': 1.0 + nrm(17, (DEPTH, D_MODEL), 0.01),
        'g_post_ffn': 1.0 + nrm(18, (DEPTH, D_MODEL), 0.01),
        's5_lambda_re': -0.5 + nrm(19, (DEPTH, G, P), 0.01),
        's5_lambda_im': math.pi * jnp.arange(P, dtype=F32) + nrm(20, (DEPTH, G, P), 0.01),
        's5_log_dt': jax.random.uniform(keys[21], (DEPTH, G), F32, math.log(1e-3), math.log(1e-1)),
        's5_b_re': nrm(22, (DEPTH, G, P, CH), (2 * CH) ** -0.5),
        's5_b_im': nrm(23, (DEPTH, G, P, CH), (2 * CH) ** -0.5),
        's5_c_re': nrm(24, (DEPTH, G, CH, P), P ** -0.5),
        's5_c_im': nrm(25, (DEPTH, G, CH, P), P ** -0.5),
        's5_d': nrm(26, (DEPTH, S5_WIDTH)),
        's5_w_glu': nrm(27, (DEPTH, S5_WIDTH, S5_WIDTH), S5_WIDTH ** -0.5),
        's5_b_glu': nrm(28, (DEPTH, S5_WIDTH), 0.01),
        'dn_conv_w': nrm(29, (DEPTH, DN_CONV, 3 * GROUP_WIDTH), DN_CONV ** -0.5),
        'dn_conv_b': nrm(30, (DEPTH, 3 * GROUP_WIDTH), 0.01),
        'dn_a_log': jnp.log(jax.random.uniform(keys[31], (DEPTH, DN_HEADS), F32, 1.0, 16.0)),
        'dn_dt_bias': dn_dt + jnp.log(-jnp.expm1(-dn_dt)),
        'dn_norm_g': 1.0 + nrm(33, (DEPTH, DN_DV), 0.01),
        'ffn_w_up': nrm(34, (DEPTH, D_MODEL, 2 * D_FF), D_MODEL ** -0.5),
        'ffn_conv_w': nrm(35, (DEPTH, FFN_CONV, 2 * D_FF), FFN_CONV ** -0.5),
        'ffn_conv_b': nrm(36, (DEPTH, 2 * D_FF), 0.01),
        'ffn_w_down': nrm(37, (DEPTH, D_FF, D_MODEL), D_FF ** -0.5),
    }


def reference(x_prompt, x_sample, cache_sb_kv, cache_moba_kv, state_s5, state_dn, state_dn_conv,
              state_ffn_conv, page_table, c_prompt, c_sample, w_in, w_out, w_ada, b_ada,
              g_pre_mix, g_post_mix, g_pre_ffn, g_post_ffn, s5_lambda_re, s5_lambda_im, s5_log_dt,
              s5_b_re, s5_b_im, s5_c_re, s5_c_im, s5_d, s5_w_glu, s5_b_glu, dn_conv_w, dn_conv_b,
              dn_a_log, dn_dt_bias, dn_norm_g, ffn_w_up, ffn_conv_w, ffn_conv_b, ffn_w_down):
    W = dict(w_in=w_in, w_out=w_out, w_ada=w_ada, b_ada=b_ada, g_pre_mix=g_pre_mix,
             g_post_mix=g_post_mix, g_pre_ffn=g_pre_ffn, g_post_ffn=g_post_ffn,
             s5_lambda_re=s5_lambda_re, s5_lambda_im=s5_lambda_im, s5_log_dt=s5_log_dt,
             s5_b_re=s5_b_re, s5_b_im=s5_b_im, s5_c_re=s5_c_re, s5_c_im=s5_c_im, s5_d=s5_d,
             s5_w_glu=s5_w_glu, s5_b_glu=s5_b_glu, dn_conv_w=dn_conv_w, dn_conv_b=dn_conv_b,
             dn_a_log=dn_a_log, dn_dt_bias=dn_dt_bias, dn_norm_g=dn_norm_g, ffn_w_up=ffn_w_up,
             ffn_conv_w=ffn_conv_w, ffn_conv_b=ffn_conv_b, ffn_w_down=ffn_w_down)
    Bp, Tp = x_prompt.shape[0], x_prompt.shape[1]
    Bs, Ts = x_sample.shape[0], x_sample.shape[1]
    past_len = page_table.shape[1] * cache_sb_kv.shape[2]
    pos_p = jnp.arange(Tp, dtype=jnp.int32)
    pos_s = past_len + jnp.arange(Ts, dtype=jnp.int32)
    s5_h0_p = jnp.zeros((Bp, S5_GROUPS, S5_STATE, 2), F32)
    dn_S0_p = jnp.zeros((Bp, DN_HEADS, DN_DK, DN_DV), F32)
    dn_conv0_p = jnp.zeros((Bp, DN_CONV - 1, 3 * GROUP_WIDTH), x_prompt.dtype)
    ffn0_p = jnp.zeros((Bp, FFN_CONV - 1, 2 * D_FF), x_prompt.dtype)

    xp, xs = x_prompt, x_sample
    st_p, st_s = [], []
    for l in range(DEPTH):
        sb_past = cache_sb_kv[l, page_table].reshape(Bs, past_len, 2, SB_HEADS, HEAD_DIM)
        mb_past = cache_moba_kv[l, page_table].reshape(Bs, past_len, 2, MOBA_HEADS, HEAD_DIM)
        xp, sp = decoder_layer(xp, c_prompt, pos_p, l, W, s5_h0_p, dn_S0_p, dn_conv0_p, ffn0_p, None, None)
        xs, ss = decoder_layer(xs, c_sample, pos_s, l, W, state_s5[l], state_dn[l], state_dn_conv[l],
                               state_ffn_conv[l], sb_past, mb_past)
        st_p.append(sp)
        st_s.append(ss)
    return (xp, xs,
            stack_layers(st_p, 0), stack_layers(st_s, 0),
            stack_layers(st_p, 1), stack_layers(st_s, 1),
            stack_layers(st_p, 2), stack_layers(st_s, 2),
            stack_layers(st_p, 3), stack_layers(st_s, 3),
            stack_layers(st_p, 4), stack_layers(st_s, 4),
            stack_layers(st_p, 5), stack_layers(st_s, 5))
```

```python
import functools

import jax
import jax.numpy as jnp
from jax import lax
from jax.experimental import pallas as pl
from jax.experimental.pallas import tpu as pltpu

F32 = jnp.float32
BF16 = jnp.bfloat16

HEAD_DIM = 128
N_HEADS = 8
GROUP_WIDTH = N_HEADS * HEAD_DIM
S5_CH = 16
S5_STATE = 64
S5_GROUPS = GROUP_WIDTH // S5_CH
S5_GT = 8
S5_NT = S5_GROUPS // S5_GT
S5_TS = S5_GT * S5_STATE
MOBA_BLOCK = 256
MOBA_TOPK = 3
DN_CHUNK = 64
N_MOD = 6
EPS = 1e-6
ATT_SCALE = HEAD_DIM ** -0.5
NEG = -1e30

LANE = 128
SUBLANE = 8
VMEM_LIMIT_BYTES = 56 * 2 ** 20

COL_S5 = 0
COL_SB = GROUP_WIDTH // LANE
COL_MB = 4 * GROUP_WIDTH // LANE
COL_DN = 7 * GROUP_WIDTH // LANE
COL_DG = 10 * GROUP_WIDTH // LANE
IN_MAIN = 11 * GROUP_WIDTH


def _cparams(*sem):
    return pltpu.CompilerParams(dimension_semantics=sem, vmem_limit_bytes=VMEM_LIMIT_BYTES)


def _dot(a, b):
    return jnp.dot(a.astype(BF16), b.astype(BF16), preferred_element_type=F32)


def _dot_nt(a, b):
    return lax.dot_general(a.astype(BF16), b.astype(BF16), (((1,), (1,)), ((), ())),
                           preferred_element_type=F32)


def _split(x):
    hi = x.astype(BF16)
    lo = (x - hi.astype(F32)).astype(BF16)
    return hi, lo


def _dot_x01(x, m01):
    hi, lo = _split(x)
    return (jnp.dot(hi, m01, preferred_element_type=F32)
            + jnp.dot(lo, m01, preferred_element_type=F32))


def _softplus(z):
    return jnp.maximum(z, 0.0) + jnp.log1p(jnp.exp(-jnp.abs(z)))


def _silu(x):
    return x * jax.nn.sigmoid(x)


def _shift_rows(x, d):
    rows = lax.broadcasted_iota(jnp.int32, x.shape, 0)
    return jnp.where(rows >= d, pltpu.roll(x, d, 0), 0.0)


def _mm_kernel(x_ref, w_ref, o_ref):
    o_ref[...] = _dot(x_ref[...], w_ref[...]).astype(o_ref.dtype)


def _mm_acc_kernel(x_ref, w_ref, o_ref, acc_ref):
    k = pl.program_id(2)
    part = _dot(x_ref[...], w_ref[...])

    @pl.when(k == 0)
    def _():
        acc_ref[...] = part

    @pl.when(k > 0)
    def _():
        acc_ref[...] += part

    @pl.when(k == pl.num_programs(2) - 1)
    def _():
        o_ref[...] = acc_ref[...].astype(o_ref.dtype)


def _matmul(x, w, *, tm, tn, tk=None, n_out=None, out_dtype=F32):
    M, K = x.shape
    n_out = w.shape[1] if n_out is None else n_out
    tm, tn = min(tm, M), min(tn, n_out)
    tk = K if tk is None else tk
    assert M % tm == 0 and n_out % tn == 0 and K % tk == 0
    nk = K // tk
    if nk == 1:
        return pl.pallas_call(
            _mm_kernel,
            grid=(M // tm, n_out // tn),
            in_specs=[pl.BlockSpec((tm, K), lambda i, j: (i, 0)),
                      pl.BlockSpec((K, tn), lambda i, j: (0, j))],
            out_specs=pl.BlockSpec((tm, tn), lambda i, j: (i, j)),
            out_shape=jax.ShapeDtypeStruct((M, n_out), out_dtype),
            compiler_params=_cparams("parallel", "parallel"),
        )(x, w)
    return pl.pallas_call(
        _mm_acc_kernel,
        grid=(M // tm, n_out // tn, nk),
        in_specs=[pl.BlockSpec((tm, tk), lambda i, j, k: (i, k)),
                  pl.BlockSpec((tk, tn), lambda i, j, k: (k, j))],
        out_specs=pl.BlockSpec((tm, tn), lambda i, j, k: (i, j)),
        out_shape=jax.ShapeDtypeStruct((M, n_out), out_dtype),
        scratch_shapes=[pltpu.VMEM((tm, tn), F32)],
        compiler_params=_cparams("parallel", "parallel", "arbitrary"),
    )(x, w)


def _ada_kernel(c_ref, w_ref, b_ref, o_ref):
    o_ref[...] = _dot(_silu(c_ref[...]), w_ref[...]) + b_ref[...]


def _ada_all(c, w_ada, b_ada, *, tn=512):
    R, D = c.shape
    L, _, N = w_ada.shape
    tn = min(tn, N)
    return pl.pallas_call(
        _ada_kernel,
        grid=(L, N // tn),
        in_specs=[pl.BlockSpec((R, D), lambda l, j: (0, 0)),
                  pl.BlockSpec((None, D, tn), lambda l, j: (l, 0, j)),
                  pl.BlockSpec((None, 1, tn), lambda l, j: (l, 0, j))],
        out_specs=pl.BlockSpec((None, R, tn), lambda l, j: (l, 0, j)),
        out_shape=jax.ShapeDtypeStruct((L, R, N), F32),
        compiler_params=_cparams("parallel", "parallel"),
    )(c, w_ada, b_ada.reshape(L, 1, N))


def _rms(x, g):
    return x * lax.rsqrt(jnp.mean(x * x, axis=-1, keepdims=True) + EPS) * g


def _prenorm_kernel(x_ref, g_ref, sc_ref, sh_ref, h_ref):
    h = _rms(x_ref[...], g_ref[...]) * (1.0 + sc_ref[...]) + sh_ref[...]
    h_ref[...] = h.astype(h_ref.dtype)


def _resid_norm_kernel(x_ref, y_ref, gate_ref, gpost_ref, gnext_ref, sc_ref, sh_ref, xo_ref, h_ref):
    x = x_ref[...] + gate_ref[...] * _rms(y_ref[...], gpost_ref[...])
    xo_ref[...] = x
    h = _rms(x, gnext_ref[...]) * (1.0 + sc_ref[...]) + sh_ref[...]
    h_ref[...] = h.astype(h_ref.dtype)


def _resid_kernel(x_ref, y_ref, gate_ref, gpost_ref, xo_ref):
    xo_ref[...] = x_ref[...] + gate_ref[...] * _rms(y_ref[...], gpost_ref[...])


def _row_specs(T, D, tr):
    act = pl.BlockSpec((None, tr, D), lambda b, i: (b, i, 0))
    per_seq = pl.BlockSpec((None, 1, D), lambda b, i: (b, 0, 0))
    gain = pl.BlockSpec((1, D), lambda b, i: (0, 0))
    return act, per_seq, gain


def _prenorm(x, g, scale, shift, *, tr=256):
    B, T, D = x.shape
    tr = min(tr, T)
    act, per_seq, gain = _row_specs(T, D, tr)
    return pl.pallas_call(
        _prenorm_kernel, grid=(B, T // tr),
        in_specs=[act, gain, per_seq, per_seq], out_specs=act,
        out_shape=jax.ShapeDtypeStruct((B, T, D), BF16),
        compiler_params=_cparams("parallel", "parallel"),
    )(x, g, scale, shift)


def _resid_norm(x, y, gate, g_post, g_next, scale, shift, *, tr=256):
    B, T, D = x.shape
    tr = min(tr, T)
    act, per_seq, gain = _row_specs(T, D, tr)
    return pl.pallas_call(
        _resid_norm_kernel, grid=(B, T // tr),
        in_specs=[act, act, per_seq, gain, gain, per_seq, per_seq], out_specs=(act, act),
        out_shape=(jax.ShapeDtypeStruct((B, T, D), F32), jax.ShapeDtypeStruct((B, T, D), BF16)),
        compiler_params=_cparams("parallel", "parallel"),
    )(x, y, gate, g_post, g_next, scale, shift)


def _resid(x, y, gate, g_post, *, tr=256):
    B, T, D = x.shape
    tr = min(tr, T)
    act, per_seq, gain = _row_specs(T, D, tr)
    return pl.pallas_call(
        _resid_kernel, grid=(B, T // tr),
        in_specs=[act, act, per_seq, gain], out_specs=act,
        out_shape=jax.ShapeDtypeStruct((B, T, D), F32),
        compiler_params=_cparams("parallel", "parallel"),
    )(x, y, gate, g_post)


def _conv_rows(x, prev_rows, w_ref, b_ref):
    K = w_ref.shape[0]
    rows = lax.broadcasted_iota(jnp.int32, x.shape, 0)
    y = b_ref[...] + x * w_ref[K - 1:K, :]
    for d in range(1, K):
        xs = pltpu.roll(x, d, 0)
        for r in range(d):
            xs = jnp.where(rows == r, prev_rows[d - r - 1], xs)
        y = y + xs * w_ref[K - 1 - d:K - d, :]
    return y


def _prev_rows(first, state_ref, tail_ref, n):
    ns, nt = state_ref.shape[0], tail_ref.shape[0]
    return [jnp.where(first, state_ref[ns - 1 - k:ns - k, :], tail_ref[nt - 1 - k:nt - k, :])
            for k in range(n)]


def _ffn_gate_kernel(a_ref, g_ref, ta_ref, tg_ref, sa_ref, sg_ref, wa_ref, wg_ref, ba_ref, bg_ref, o_ref):
    first = pl.program_id(1) == 0
    K = wa_ref.shape[0]
    a = _conv_rows(a_ref[...], _prev_rows(first, sa_ref, ta_ref, K - 1), wa_ref, ba_ref)
    g = _conv_rows(g_ref[...], _prev_rows(first, sg_ref, tg_ref, K - 1), wg_ref, bg_ref)
    o_ref[...] = (_silu(g) * a).astype(o_ref.dtype)


def _ffn_gate(up, prev, conv_w, conv_b, *, tt=128, tc=5504):
    B, T, F2 = up.shape
    F = F2 // 2
    K = conv_w.shape[0]
    tt, tc = min(tt, T), min(tc, F)
    assert T % tt == 0 and F % tc == 0 and tt % SUBLANE == 0
    nc = F // tc
    tb = tt // SUBLANE
    cur = lambda off: pl.BlockSpec((None, tt, tc), lambda b, i, j: (b, i, j + off))
    tail = lambda off: pl.BlockSpec((None, SUBLANE, tc), lambda b, i, j: (b, jnp.maximum(i * tb - 1, 0), j + off))
    st = lambda off: pl.BlockSpec((None, K - 1, tc), lambda b, i, j: (b, 0, j + off))
    wsp = lambda off: pl.BlockSpec((K, tc), lambda b, i, j: (0, j + off))
    bsp = lambda off: pl.BlockSpec((1, tc), lambda b, i, j: (0, j + off))
    return pl.pallas_call(
        _ffn_gate_kernel, grid=(B, T // tt, nc),
        in_specs=[cur(0), cur(nc), tail(0), tail(nc), st(0), st(nc), wsp(0), wsp(nc), bsp(0), bsp(nc)],
        out_specs=pl.BlockSpec((None, tt, tc), lambda b, i, j: (b, i, j)),
        out_shape=jax.ShapeDtypeStruct((B, T, F), BF16),
        compiler_params=_cparams("parallel", "parallel", "parallel"),
    )(up, up, up, up, prev, prev, conv_w, conv_w, conv_b.reshape(1, F2), conv_b.reshape(1, F2))


def _ffn_gate_step_kernel(up_ref, prev_ref, w_ref, b_ref, o_ref):
    F = o_ref.shape[1]
    K = w_ref.shape[0]
    y = b_ref[...] + up_ref[...] * w_ref[K - 1:K, :]
    for k in range(K - 1):
        y = y + prev_ref[:, k, :] * w_ref[k:k + 1, :]
    o_ref[...] = (_silu(y[:, F:]) * y[:, :F]).astype(o_ref.dtype)


def _ffn_gate_step(up, prev, conv_w, conv_b):
    B, F2 = up.shape
    return pl.pallas_call(
        _ffn_gate_step_kernel,
        out_shape=jax.ShapeDtypeStruct((B, F2 // 2), BF16),
        compiler_params=pltpu.CompilerParams(vmem_limit_bytes=VMEM_LIMIT_BYTES),
    )(up, prev, conv_w, conv_b.reshape(1, F2))


def _s5_param_kernel(lr_ref, li_ref, dt_ref, br_ref, bi_ref, ar_ref, ai_ref, bbr_ref, bbi_ref):
    lr, li = lr_ref[...], li_ref[...]
    dt = jnp.exp(dt_ref[...])
    mag = jnp.exp(lr * dt)
    ar, ai = mag * jnp.cos(li * dt), mag * jnp.sin(li * dt)
    den = lr * lr + li * li
    fr = ((ar - 1.0) * lr + ai * li) / den
    fi = (ai * lr - (ar - 1.0) * li) / den
    ar_ref[...], ai_ref[...] = ar, ai
    bbr_ref[...] = fr * br_ref[...] - fi * bi_ref[...]
    bbi_ref[...] = fr * bi_ref[...] + fi * br_ref[...]


def _s5_params(lam_re, lam_im, log_dt, b_re, b_im):
    G, P = lam_re.shape
    CH = b_re.shape[-1]
    col = lambda a: a.reshape(G * P, 1)
    dt = jnp.broadcast_to(log_dt[:, None], (G, P))
    return pl.pallas_call(
        _s5_param_kernel,
        out_shape=(jax.ShapeDtypeStruct((G * P, 1), F32),) * 2 + (jax.ShapeDtypeStruct((G * P, CH), F32),) * 2,
    )(col(lam_re), col(lam_im), col(dt), b_re.reshape(G * P, CH), b_im.reshape(G * P, CH))


def _s5_pack(ar, ai, bbr, bbi, c_re, c_im):
    eye = jnp.eye(S5_GT, dtype=F32)
    a_re = ar.reshape(S5_NT, 1, S5_TS)
    a_im = ai.reshape(S5_NT, 1, S5_TS)

    def b_dense(bb):
        bb = bb.reshape(S5_NT, S5_GT, S5_STATE, S5_CH)
        return jnp.einsum('tgpc,gh->tgchp', bb, eye).reshape(S5_NT, S5_GT * S5_CH, S5_TS)

    def c_dense(c):
        c = c.reshape(S5_NT, S5_GT, S5_CH, S5_STATE)
        return jnp.einsum('tgcp,gh->tgphc', c, eye).reshape(S5_NT, S5_TS, S5_GT * S5_CH)

    bd = jnp.concatenate([b_dense(bbr), b_dense(bbi)], axis=-1).astype(BF16)
    return a_re, a_im, bd, c_dense(c_re).astype(BF16), c_dense(c_im).astype(BF16)


def _cmul(ar, ai, br, bi):
    return ar * br - ai * bi, ar * bi + ai * br


def _s5_kernel(u_ref, h0_ref, are_ref, aim_ref, bd_ref, cre_ref, cim_ref, d_ref, y_ref, hs_ref,
               pr_ref, pi_ref, carry_ref):
    c = pl.program_id(2)
    L = u_ref.shape[0]

    @pl.when(c == 0)
    def _():
        pr_ref[0:1, :] = are_ref[...]
        pi_ref[0:1, :] = aim_ref[...]
        n = 1
        while n < L:
            tr, ti = pr_ref[n - 1:n, :], pi_ref[n - 1:n, :]
            qr, qi = _cmul(pr_ref[0:n, :], pi_ref[0:n, :], tr, ti)
            pr_ref[n:2 * n, :] = qr
            pi_ref[n:2 * n, :] = qi
            n *= 2
        carry_ref[...] = h0_ref[...]

    u = u_ref[...]
    bu = _dot(u, bd_ref[...])
    hr, hi = bu[:, :S5_TS], bu[:, S5_TS:]
    d = 1
    while d < L:
        tr, ti = pr_ref[d - 1:d, :], pi_ref[d - 1:d, :]
        sr, si = _cmul(tr, ti, _shift_rows(hr, d), _shift_rows(hi, d))
        hr, hi = hr + sr, hi + si
        d *= 2
    cr, ci = _cmul(pr_ref[...], pi_ref[...], carry_ref[0:1, :], carry_ref[1:2, :])
    hr, hi = hr + cr, hi + ci
    carry_ref[0:1, :] = hr[L - 1:L, :]
    carry_ref[1:2, :] = hi[L - 1:L, :]
    y = _dot(hr, cre_ref[...]) - _dot(hi, cim_ref[...]) + u * d_ref[...]
    y_ref[...] = jax.nn.gelu(y)

    @pl.when(c == pl.num_programs(2) - 1)
    def _():
        hs_ref[...] = carry_ref[...]


def _s5_scan(proj, h0, packed, d_skip, *, L=256):
    B, T, _ = proj.shape
    a_re, a_im, bd, cre, cim = packed
    L = min(L, T)
    assert T % L == 0 and L & (L - 1) == 0
    tile = lambda t: pl.BlockSpec((None,) + t, lambda b, g, c: (g, 0, 0))
    return pl.pallas_call(
        _s5_kernel, grid=(B, S5_NT, T // L),
        in_specs=[pl.BlockSpec((None, L, LANE), lambda b, g, c: (b, c, COL_S5 + g)),
                  pl.BlockSpec((None, 2, S5_TS), lambda b, g, c: (b, 0, g)),
                  tile((1, S5_TS)), tile((1, S5_TS)), tile((LANE, 2 * S5_TS)),
                  tile((S5_TS, LANE)), tile((S5_TS, LANE)),
                  pl.BlockSpec((1, LANE), lambda b, g, c: (0, g))],
        out_specs=(pl.BlockSpec((None, L, LANE), lambda b, g, c: (b, c, g)),
                   pl.BlockSpec((None, 2, S5_TS), lambda b, g, c: (b, 0, g))),
        out_shape=(jax.ShapeDtypeStruct((B, T, GROUP_WIDTH), F32),
                   jax.ShapeDtypeStruct((B, 2, S5_GROUPS * S5_STATE), F32)),
        scratch_shapes=[pltpu.VMEM((L, S5_TS), F32), pltpu.VMEM((L, S5_TS), F32),
                        pltpu.VMEM((2, S5_TS), F32)],
        compiler_params=_cparams("parallel", "parallel", "arbitrary"),
    )(proj, h0, a_re, a_im, bd, cre, cim, d_skip.reshape(1, GROUP_WIDTH))


def _s5_step_kernel(u_ref, h0_ref, are_ref, aim_ref, bd_ref, cre_ref, cim_ref, d_ref, y_ref, hs_ref):
    for g in range(S5_NT):
        u = u_ref[:, g * LANE:(g + 1) * LANE]
        bu = _dot(u, bd_ref[g])
        sl = slice(g * S5_TS, (g + 1) * S5_TS)
        ar, ai = are_ref[g], aim_ref[g]
        pr, pi = _cmul(ar, ai, h0_ref[:, 0, sl], h0_ref[:, 1, sl])
        hr, hi = bu[:, :S5_TS] + pr, bu[:, S5_TS:] + pi
        hs_ref[:, 0, sl] = hr
        hs_ref[:, 1, sl] = hi
        y = _dot(hr, cre_ref[g]) - _dot(hi, cim_ref[g]) + u * d_ref[:, g * LANE:(g + 1) * LANE]
        y_ref[:, g * LANE:(g + 1) * LANE] = jax.nn.gelu(y)


def _s5_step(u, h0, packed, d_skip):
    B = u.shape[0]
    a_re, a_im, bd, cre, cim = packed
    return pl.pallas_call(
        _s5_step_kernel,
        out_shape=(jax.ShapeDtypeStruct((B, GROUP_WIDTH), F32),
                   jax.ShapeDtypeStruct((B, 2, S5_GROUPS * S5_STATE), F32)),
        compiler_params=pltpu.CompilerParams(vmem_limit_bytes=VMEM_LIMIT_BYTES),
    )(u, h0, a_re, a_im, bd, cre, cim, d_skip.reshape(1, GROUP_WIDTH))


def _glu_kernel(y_ref, w_ref, b_ref, o_ref):
    y = y_ref[...]
    o_ref[...] = (y * jax.nn.sigmoid(_dot(y, w_ref[...]) + b_ref[...])).astype(o_ref.dtype)


def _glu(y, w, b, *, tm=1024):
    M, N = y.shape
    tm = min(tm, M)
    return pl.pallas_call(
        _glu_kernel, grid=(M // tm,),
        in_specs=[pl.BlockSpec((tm, N), lambda i: (i, 0)),
                  pl.BlockSpec((N, N), lambda i: (0, 0)),
                  pl.BlockSpec((1, N), lambda i: (0, 0))],
        out_specs=pl.BlockSpec((tm, N), lambda i: (i, 0)),
        out_shape=jax.ShapeDtypeStruct((M, N), BF16),
        compiler_params=_cparams("parallel"),
    )(y, w, b.reshape(1, N))


def _sb_block(q, kb, vb, upper, run, acc, mask):
    z = _dot_nt(q, kb) * ATT_SCALE
    sp = _softplus(z)
    log_keep = -sp if mask is None else jnp.where(mask, -sp, 0.0)
    later = _dot_x01(log_keep, upper)
    w = jnp.exp(z - sp + later + run)
    if mask is not None:
        w = jnp.where(mask, w, 0.0)
    acc = acc + _dot(w, vb)
    run = run + later[:, 0:1] + log_keep[:, 0:1]
    return run, acc


def _sb_kernel(q_ref, k_ref, v_ref, o_ref):
    i = pl.program_id(2)
    tq = q_ref.shape[0]
    q = q_ref[...]
    r = lax.broadcasted_iota(jnp.int32, (tq, tq), 0)
    c = lax.broadcasted_iota(jnp.int32, (tq, tq), 1)
    upper = jnp.where(r > c, 1.0, 0.0).astype(BF16)
    diag = pl.multiple_of(i * tq, tq)
    run, acc = _sb_block(q, k_ref[pl.ds(diag, tq), :], v_ref[pl.ds(diag, tq), :], upper,
                         jnp.zeros((tq, 1), F32), jnp.zeros((tq, HEAD_DIM), F32), c < r)

    def body(jj, carry):
        start = pl.multiple_of((i - 1 - jj) * tq, tq)
        return _sb_block(q, k_ref[pl.ds(start, tq), :], v_ref[pl.ds(start, tq), :], upper,
                         carry[0], carry[1], None)

    run, acc = lax.fori_loop(0, i, body, (run, acc))
    o_ref[...] = acc.astype(o_ref.dtype)


def _sb_attention(proj, *, tq=256):
    B, T, _ = proj.shape
    tq = min(tq, T)
    assert T % tq == 0
    seq = lambda off: pl.BlockSpec((None, T, HEAD_DIM), lambda b, h, i: (b, 0, off + h))
    return pl.pallas_call(
        _sb_kernel, grid=(B, N_HEADS, T // tq),
        in_specs=[pl.BlockSpec((None, tq, HEAD_DIM), lambda b, h, i: (b, i, COL_SB + h)),
                  seq(COL_SB + N_HEADS), seq(COL_SB + 2 * N_HEADS)],
        out_specs=pl.BlockSpec((None, tq, HEAD_DIM), lambda b, h, i: (b, i, h)),
        out_shape=jax.ShapeDtypeStruct((B, T, GROUP_WIDTH), BF16),
        compiler_params=_cparams("parallel", "parallel", "parallel"),
    )(proj, proj, proj)


def _head_layout(page_rows):
    cols = page_rows * N_HEADS
    r = lax.broadcasted_iota(jnp.int32, (N_HEADS, cols), 0)
    c = lax.broadcasted_iota(jnp.int32, (N_HEADS, cols), 1)
    own = (c % N_HEADS) == r
    tok = c // N_HEADS
    return own, tok


def _lane_chunks(x):
    return jnp.concatenate([x[:, j * LANE:(j + 1) * LANE] for j in range(x.shape[1] // LANE)], axis=0)


def _later_sums(x, upper):
    n = x.shape[1] // LANE
    within = _dot_x01(_lane_chunks(x), upper)
    out = [None] * n
    tail = jnp.zeros((N_HEADS, 1), F32)
    for j in reversed(range(n)):
        wj = within[j * N_HEADS:(j + 1) * N_HEADS, :]
        out[j] = wj + tail
        tail = tail + wj[:, 0:1] + x[:, j * LANE:j * LANE + 1]
    return jnp.concatenate(out, axis=1), tail


def _sb_step_kernel(pt_ref, q_ref, kv_ref, o_ref, run_ref, acc_ref):
    s = pl.program_id(1)
    P = kv_ref.shape[0]

    @pl.when(s == 0)
    def _():
        run_ref[...] = jnp.zeros_like(run_ref)
        acc_ref[...] = jnp.zeros_like(acc_ref)

    own, _ = _head_layout(P)
    r = lax.broadcasted_iota(jnp.int32, (LANE, LANE), 0)
    c = lax.broadcasted_iota(jnp.int32, (LANE, LANE), 1)
    upper = jnp.where(r > c, 1.0, 0.0).astype(BF16)
    k2 = kv_ref[:, 0, :, :].reshape(P * N_HEADS, HEAD_DIM)
    v2 = kv_ref[:, 1, :, :].reshape(P * N_HEADS, HEAD_DIM)
    z = _dot_nt(q_ref[...], k2) * ATT_SCALE
    sp = _softplus(z)
    log_keep = jnp.where(own, -sp, 0.0)
    later, total = _later_sums(log_keep, upper)
    run = run_ref[:, 0:1]
    w = jnp.where(own, jnp.exp(z - sp + later + run), 0.0)
    acc_ref[...] += _dot(w, v2)
    run_ref[...] = jnp.broadcast_to(run + total, run_ref.shape)

    @pl.when(s == pl.num_programs(1) - 1)
    def _():
        o_ref[...] = acc_ref[...]


def _sb_step(q, cache, page_table, layer):
    B = q.shape[0]
    n_pages = page_table.shape[1]
    P = cache.shape[2]
    pt = jnp.concatenate([page_table.reshape(-1), jnp.reshape(layer, (1,)).astype(jnp.int32)])
    return pl.pallas_call(
        _sb_step_kernel,
        grid_spec=pltpu.PrefetchScalarGridSpec(
            num_scalar_prefetch=1, grid=(B, n_pages),
            in_specs=[pl.BlockSpec((None, N_HEADS, HEAD_DIM), lambda b, s, pt: (b, 0, 0)),
                      pl.BlockSpec((None, None, P, 2, N_HEADS, HEAD_DIM),
                                   lambda b, s, pt: (pt[B * n_pages], pt[b * n_pages + n_pages - 1 - s], 0, 0, 0, 0))],
            out_specs=pl.BlockSpec((None, N_HEADS, HEAD_DIM), lambda b, s, pt: (b, 0, 0)),
            scratch_shapes=[pltpu.VMEM((N_HEADS, LANE), F32), pltpu.VMEM((N_HEADS, HEAD_DIM), F32)]),
        out_shape=jax.ShapeDtypeStruct((B, N_HEADS, HEAD_DIM), F32),
        compiler_params=_cparams("parallel", "arbitrary"),
    )(pt, q, cache)


def _dot_nt_precise(a, b):
    ah, al = _split(a)
    bh, bl = _split(b)
    return _dot_nt(ah, bh) + _dot_nt(ah, bl) + _dot_nt(al, bh)


def _topk_mask(gate, n_valid, k):
    n = gate.shape[1]
    idx = lax.broadcasted_iota(jnp.int32, gate.shape, 1)
    rank = jnp.zeros(gate.shape, F32)
    for j in range(n):
        gj = gate[:, j:j + 1]
        ahead = (gj > gate) | ((gj == gate) & (j < idx))
        rank = rank + jnp.where(ahead, jnp.where(j < n_valid, 1.0, 0.0), 0.0)
    return jnp.where(idx < n_valid, jnp.where(rank < k, 1.0, 0.0), 0.0)


def _moba_kernel(slope_ref, q_ref, k_ref, v_ref, o_ref, sel_ref):
    h, i = pl.program_id(1), pl.program_id(2)
    bs = q_ref.shape[0]
    nb = k_ref.shape[0] // bs
    slope = slope_ref[h]
    q = q_ref[...]
    kmean = jnp.concatenate(
        [jnp.mean(k_ref[j * bs:(j + 1) * bs, :], axis=0, keepdims=True) for j in range(nb)], axis=0)
    gate = _dot_nt_precise(q, kmean)
    sel = _topk_mask(gate, i, MOBA_TOPK)
    for j in range(nb):
        sel_ref[j] = sel[:, j:j + 1]

    r = lax.broadcasted_iota(jnp.int32, (bs, bs), 0)
    c = lax.broadcasted_iota(jnp.int32, (bs, bs), 1)
    rel = (r - c).astype(F32)
    own = pl.multiple_of(i * bs, bs)
    s = _dot_nt(q, k_ref[pl.ds(own, bs), :]) * ATT_SCALE - slope * rel
    s = jnp.where(c <= r, s, NEG)
    m = jnp.max(s, axis=-1, keepdims=True)
    p = jnp.exp(s - m)
    l = jnp.sum(p, axis=-1, keepdims=True)
    acc = _dot(p, v_ref[pl.ds(own, bs), :])

    def body(j, carry):
        m, l, acc = carry
        start = pl.multiple_of(j * bs, bs)
        dist = rel + ((i - j) * bs).astype(F32)
        s = _dot_nt(q, k_ref[pl.ds(start, bs), :]) * ATT_SCALE - slope * dist
        s = jnp.where(sel_ref[j] > 0.0, s, NEG)
        m_new = jnp.maximum(m, jnp.max(s, axis=-1, keepdims=True))
        a = jnp.exp(m - m_new)
        p = jnp.exp(s - m_new)
        return m_new, a * l + jnp.sum(p, axis=-1, keepdims=True), a * acc + _dot(p, v_ref[pl.ds(start, bs), :])

    m, l, acc = lax.fori_loop(0, i, body, (m, l, acc))
    o_ref[...] = (acc / l).astype(o_ref.dtype)


def _moba_attention(proj, slopes):
    B, T, _ = proj.shape
    bs = min(MOBA_BLOCK, T)
    assert T % bs == 0
    nb = T // bs
    seq = lambda off: pl.BlockSpec((None, T, HEAD_DIM), lambda b, h, i, sl: (b, 0, off + h))
    return pl.pallas_call(
        _moba_kernel,
        grid_spec=pltpu.PrefetchScalarGridSpec(
            num_scalar_prefetch=1, grid=(B, N_HEADS, nb),
            in_specs=[pl.BlockSpec((None, bs, HEAD_DIM), lambda b, h, i, sl: (b, i, COL_MB + h)),
                      seq(COL_MB + N_HEADS), seq(COL_MB + 2 * N_HEADS)],
            out_specs=pl.BlockSpec((None, bs, HEAD_DIM), lambda b, h, i, sl: (b, i, h)),
            scratch_shapes=[pltpu.VMEM((nb, bs, 1), F32)]),
        out_shape=jax.ShapeDtypeStruct((B, T, GROUP_WIDTH), BF16),
        compiler_params=_cparams("parallel", "parallel", "parallel"),
    )(slopes, proj, proj, proj)


def _moba_step_kernel(pt_ref, q_ref, ks_ref, vs_ref, slope_ref, kv_ref, o_ref, m_ref, l_ref, acc_ref, g_ref):
    s = pl.program_id(1)
    n_pages = pl.num_programs(1)
    P = kv_ref.shape[0]
    own, tok = _head_layout(P)
    q = q_ref[...]
    slope = slope_ref[...]
    k3 = kv_ref[:, 0, :, :]
    k2 = k3.reshape(P * N_HEADS, HEAD_DIM)
    v2 = kv_ref[:, 1, :, :].reshape(P * N_HEADS, HEAD_DIM)
    dist = ((n_pages - s) * P).astype(F32) - tok.astype(F32)
    z = _dot_nt(q, k2) * ATT_SCALE - slope * dist
    z = jnp.where(own, z, NEG)
    m = jnp.max(z, axis=-1, keepdims=True)
    p = jnp.where(own, jnp.exp(z - m), 0.0)
    m_ref[s] = jnp.broadcast_to(m, (N_HEADS, LANE))
    l_ref[s] = jnp.broadcast_to(jnp.sum(p, axis=-1, keepdims=True), (N_HEADS, LANE))
    acc_ref[s] = _dot(p, v2)
    ksum = jnp.sum(k3, axis=0)
    g_ref[s] = jnp.broadcast_to(jnp.sum(q * ksum, axis=-1, keepdims=True), (N_HEADS, LANE))

    @pl.when(s == n_pages - 1)
    def _():
        ppb = MOBA_BLOCK // P
        nb = g_ref.shape[0] // ppb
        gate = []
        for j in range(nb):
            gj = g_ref[j * ppb]
            for t in range(1, ppb):
                gj = gj + g_ref[j * ppb + t]
            gate.append(gj * (1.0 / MOBA_BLOCK))
        sel = []
        for j in range(nb):
            rank = jnp.zeros((N_HEADS, LANE), F32)
            for t in range(nb):
                ahead = (gate[t] > gate[j]) if t > j else (gate[t] >= gate[j])
                if t != j:
                    rank = rank + jnp.where(ahead, 1.0, 0.0)
            sel.append(rank < MOBA_TOPK)
        s_self = jnp.sum(q * ks_ref[...], axis=-1, keepdims=True) * ATT_SCALE
        mx = jnp.broadcast_to(s_self, (N_HEADS, LANE))
        for j in range(nb):
            for t in range(ppb):
                mx = jnp.where(sel[j], jnp.maximum(mx, m_ref[j * ppb + t]), mx)
        e_self = jnp.exp(s_self - mx)
        den = e_self
        num = e_self * vs_ref[...]
        for j in range(nb):
            for t in range(ppb):
                wgt = jnp.where(sel[j], jnp.exp(m_ref[j * ppb + t] - mx), 0.0)
                den = den + wgt * l_ref[j * ppb + t]
                num = num + wgt * acc_ref[j * ppb + t]
        o_ref[...] = num / den


def _moba_step(q, k_self, v_self, cache, page_table, layer, slopes):
    B = q.shape[0]
    n_pages = page_table.shape[1]
    P = cache.shape[2]
    assert MOBA_BLOCK % P == 0 and (n_pages * P) % MOBA_BLOCK == 0
    pt = jnp.concatenate([page_table.reshape(-1), jnp.reshape(layer, (1,)).astype(jnp.int32)])
    head = pl.BlockSpec((None, N_HEADS, HEAD_DIM), lambda b, s, pt: (b, 0, 0))
    per_page = pltpu.VMEM((n_pages, N_HEADS, LANE), F32)
    return pl.pallas_call(
        _moba_step_kernel,
        grid_spec=pltpu.PrefetchScalarGridSpec(
            num_scalar_prefetch=1, grid=(B, n_pages),
            in_specs=[head, head, head,
                      pl.BlockSpec((N_HEADS, 1), lambda b, s, pt: (0, 0)),
                      pl.BlockSpec((None, None, P, 2, N_HEADS, HEAD_DIM),
                                   lambda b, s, pt: (pt[B * n_pages], pt[b * n_pages + s], 0, 0, 0, 0))],
            out_specs=head,
            scratch_shapes=[per_page, per_page, per_page, per_page]),
        out_shape=jax.ShapeDtypeStruct((B, N_HEADS, HEAD_DIM), F32),
        compiler_params=_cparams("parallel", "arbitrary"),
    )(pt, q, k_self, v_self, slopes.reshape(N_HEADS, 1), cache)


def _l2n(x):
    return x * lax.rsqrt(jnp.sum(x * x, axis=-1, keepdims=True) + EPS)


def _dn_prep_kernel(x_ref, tail_ref, st_ref, w_ref, b_ref, o_ref):
    part = pl.program_id(2)
    first = pl.program_id(1) == 0
    K = w_ref.shape[0]
    y = _silu(_conv_rows(x_ref[...], _prev_rows(first, st_ref, tail_ref, K - 1), w_ref, b_ref))

    @pl.when(part == 2)
    def _():
        o_ref[...] = y

    @pl.when(part < 2)
    def _():
        scale = jnp.where(part == 0, HEAD_DIM ** -0.5, 1.0)
        for hh in range(N_HEADS):
            sl = slice(hh * HEAD_DIM, (hh + 1) * HEAD_DIM)
            o_ref[:, sl] = _l2n(y[:, sl]) * scale


def _dn_prep(proj, prev, conv_w, conv_b, *, tt=256):
    B, T, _ = proj.shape
    K = conv_w.shape[0]
    GW = GROUP_WIDTH
    tt = min(tt, T)
    assert T % tt == 0 and tt % SUBLANE == 0
    tb = tt // SUBLANE
    cw = COL_DN * LANE // GW
    return pl.pallas_call(
        _dn_prep_kernel, grid=(B, T // tt, 3),
        in_specs=[pl.BlockSpec((None, tt, GW), lambda b, i, p: (b, i, cw + p)),
                  pl.BlockSpec((None, SUBLANE, GW), lambda b, i, p: (b, jnp.maximum(i * tb - 1, 0), cw + p)),
                  pl.BlockSpec((None, K - 1, GW), lambda b, i, p: (b, 0, p)),
                  pl.BlockSpec((K, GW), lambda b, i, p: (0, p)),
                  pl.BlockSpec((1, GW), lambda b, i, p: (0, p))],
        out_specs=pl.BlockSpec((None, tt, GW), lambda b, i, p: (b, i, p)),
        out_shape=jax.ShapeDtypeStruct((B, T, 3 * GW), F32),
        compiler_params=_cparams("parallel", "parallel", "parallel"),
    )(proj, proj, prev, conv_w, conv_b.reshape(1, 3 * GW))


def _dn_gate_values(pe, alog_ref, dtb_ref):
    col = lax.broadcasted_iota(jnp.int32, pe.shape, 1)
    g = -jnp.exp(alog_ref[...]) * _softplus(pe + dtb_ref[...])
    return col, jnp.where(col < N_HEADS, jax.nn.sigmoid(pe), g)


def _dn_gates_kernel(pe_ref, alog_ref, dtb_ref, o_ref):
    col, val = _dn_gate_values(pe_ref[...], alog_ref, dtb_ref)
    rows = lax.broadcasted_iota(jnp.int32, val.shape, 0)
    dec = jnp.where(col < N_HEADS, 0.0, val)
    d = 1
    while d < min(DN_CHUNK, val.shape[0]):
        dec = dec + jnp.where((rows % DN_CHUNK) >= d, pltpu.roll(dec, d, 0), 0.0)
        d *= 2
    o_ref[...] = jnp.where(col < N_HEADS, val, dec)


def _gate_rows(a_log, dt_bias):
    pad = lambda a: jnp.zeros((1, LANE), F32).at[0, N_HEADS:2 * N_HEADS].set(a)
    return pad(a_log), pad(dt_bias)


def _dn_gates(pe, a_log, dt_bias, *, tt=512):
    B, T, _ = pe.shape
    tt = min(tt, T)
    assert T % tt == 0 and (tt % DN_CHUNK == 0 or tt == T)
    alog, dtb = _gate_rows(a_log, dt_bias)
    row = pl.BlockSpec((1, LANE), lambda b, i: (0, 0))
    blk = pl.BlockSpec((None, tt, LANE), lambda b, i: (b, i, 0))
    return pl.pallas_call(
        _dn_gates_kernel, grid=(B, T // tt), in_specs=[blk, row, row], out_specs=blk,
        out_shape=jax.ShapeDtypeStruct((B, T, LANE), F32),
        compiler_params=_cparams("parallel", "parallel"),
    )(pe, alog, dtb)


def _pick_col(x, j):
    col = lax.broadcasted_iota(jnp.int32, x.shape, 1)
    return jnp.sum(jnp.where(col == j, x, 0.0), axis=-1, keepdims=True)


def _dot_precise(a, b):
    ah, al = _split(a)
    bh, bl = _split(b)
    return (jnp.dot(ah, bh, preferred_element_type=F32) + jnp.dot(ah, bl, preferred_element_type=F32)
            + jnp.dot(al, bh, preferred_element_type=F32))


SOLVE_BLOCK = 16


def _unit_lower_solve(a, rhs):
    C = a.shape[0]
    r = lax.broadcasted_iota(jnp.int32, (C, C), 0)
    c = lax.broadcasted_iota(jnp.int32, (C, C), 1)
    in_diag = (r // SOLVE_BLOCK) == (c // SOLVE_BLOCK)
    a_off = jnp.where(in_diag, 0.0, a)
    p = jnp.where(in_diag, -a, 0.0)
    t = jnp.where(r == c, 1.0, 0.0) + p
    n = 2
    while n < min(SOLVE_BLOCK, C):
        p = _dot_precise(p, p)
        t = t + _dot_precise(p, t)
        n *= 2
    x = _dot_precise(t, rhs)
    for _ in range(-(-C // SOLVE_BLOCK) - 1):
        x = _dot_precise(t, rhs - _dot_precise(a_off, x))
    return x


def _dn_chunk(q, k, v, beta, dcol, drow, S):
    C = q.shape[0]
    r = lax.broadcasted_iota(jnp.int32, (C, C), 0)
    c = lax.broadcasted_iota(jnp.int32, (C, C), 1)
    lmat = jnp.where(r >= c, jnp.exp(jnp.where(r >= c, dcol - drow, 0.0)), 0.0)
    kb = k * beta
    a = jnp.where(r > c, _dot_nt(kb, k) * lmat, 0.0)
    x = _unit_lower_solve(a, jnp.concatenate([v * beta, kb * jnp.exp(dcol)], axis=1))
    u, w = x[:, :HEAD_DIM], x[:, HEAD_DIM:]
    qk = _dot_nt(q, k) * lmat
    v_new = u - _dot(w, S)
    o = _dot(q * jnp.exp(dcol), S) + _dot(qk, v_new)
    d_last = dcol[C - 1:C, :]
    kd = k * jnp.exp(d_last - dcol)
    S = S * jnp.exp(d_last) + _dot(kd.T, v_new)
    return o, S


def _dn_kernel(q_ref, k_ref, v_ref, gate_ref, bd_ref, bdt_ref, s0_ref, ng_ref, o_ref, s_ref, *, hb):
    hg = pl.program_id(1)
    T = q_ref.shape[0]
    C = min(DN_CHUNK, T)

    def body(ci, states):
        r0 = pl.multiple_of(ci * C, C)
        bd = bd_ref[pl.ds(r0, C), :]
        out = []
        for hh in range(hb):
            sl = slice(hh * HEAD_DIM, (hh + 1) * HEAD_DIM)
            head = hg * hb + hh
            beta = _pick_col(bd, head)
            dcol = _pick_col(bd, N_HEADS + head)
            drow = bdt_ref[hh, pl.ds(ci, 1), :]
            o, S = _dn_chunk(q_ref[pl.ds(r0, C), sl], k_ref[pl.ds(r0, C), sl], v_ref[pl.ds(r0, C), sl],
                             beta, dcol, drow, states[hh])
            y = _rms(o, ng_ref[...]) * _silu(gate_ref[pl.ds(r0, C), sl])
            o_ref[pl.ds(r0, C), sl] = y.astype(o_ref.dtype)
            out.append(S)
        return tuple(out)

    states = lax.fori_loop(0, T // C, body, tuple(s0_ref[hh] for hh in range(hb)))
    for hh in range(hb):
        s_ref[hh] = states[hh]


def _dn_attention(qkv, proj, bd, S0, norm_g, *, hb=2):
    B, T, _ = qkv.shape
    C = min(DN_CHUNK, T)
    n = T // C
    W = hb * HEAD_DIM
    ng = N_HEADS // hb
    bdt = jnp.swapaxes(bd[:, :, N_HEADS:2 * N_HEADS], 1, 2).reshape(B, N_HEADS, n, C)
    seq = lambda off: pl.BlockSpec((None, T, W), lambda b, g: (b, 0, off + g))
    st = pl.BlockSpec((None, hb, HEAD_DIM, HEAD_DIM), lambda b, g: (b, g, 0, 0))
    return pl.pallas_call(
        functools.partial(_dn_kernel, hb=hb), grid=(B, ng),
        in_specs=[seq(0), seq(ng), seq(2 * ng),
                  pl.BlockSpec((None, T, W), lambda b, g: (b, 0, COL_DG * LANE // W + g)),
                  pl.BlockSpec((None, T, LANE), lambda b, g: (b, 0, 0)),
                  pl.BlockSpec((None, hb, n, C), lambda b, g: (b, g, 0, 0)),
                  st, pl.BlockSpec((1, HEAD_DIM), lambda b, g: (0, 0))],
        out_specs=(pl.BlockSpec((None, T, W), lambda b, g: (b, 0, g)), st),
        out_shape=(jax.ShapeDtypeStruct((B, T, GROUP_WIDTH), BF16),
                   jax.ShapeDtypeStruct((B, N_HEADS, HEAD_DIM, HEAD_DIM), F32)),
        compiler_params=_cparams("parallel", "parallel"),
    )(qkv, qkv, qkv, proj, bd, bdt, S0, norm_g.reshape(1, HEAD_DIM))


def _dn_step_prep_kernel(x_ref, prev_ref, w_ref, b_ref, pe_ref, alog_ref, dtb_ref, o_ref, bg_ref):
    K = w_ref.shape[0]
    y = b_ref[...] + x_ref[...] * w_ref[K - 1:K, :]
    for k in range(K - 1):
        y = y + prev_ref[:, k, :] * w_ref[k:k + 1, :]
    y = _silu(y)
    for part in range(3):
        for hh in range(N_HEADS):
            lo = part * GROUP_WIDTH + hh * HEAD_DIM
            seg = y[:, lo:lo + HEAD_DIM]
            if part == 0:
                seg = _l2n(seg) * HEAD_DIM ** -0.5
            elif part == 1:
                seg = _l2n(seg)
            o_ref[:, lo:lo + HEAD_DIM] = seg
    _, bg_ref[...] = _dn_gate_values(pe_ref[...], alog_ref, dtb_ref)


def _dn_step_prep(x, prev, conv_w, conv_b, pe, a_log, dt_bias):
    B, W = x.shape
    alog, dtb = _gate_rows(a_log, dt_bias)
    return pl.pallas_call(
        _dn_step_prep_kernel,
        out_shape=(jax.ShapeDtypeStruct((B, W), F32), jax.ShapeDtypeStruct((B, LANE), F32)),
    )(x, prev, conv_w, conv_b.reshape(1, W), pe, alog, dtb)


def _dn_step_kernel(q_ref, k_ref, v_ref, gate_ref, bg_ref, s0_ref, ng_ref, o_ref, s_ref):
    h = pl.program_id(1)
    q, k, v = q_ref[...], k_ref[...], v_ref[...]
    bg = bg_ref[...]
    beta = _pick_col(bg, h)
    e = jnp.exp(_pick_col(bg, N_HEADS + h))
    S = s0_ref[...]
    rows = jnp.concatenate([k * beta * e, q * e, jnp.zeros((SUBLANE - 2, HEAD_DIM), F32)], axis=0)
    rs = _dot(rows, S)
    v_new = v * beta - rs[0:1, :]
    o = rs[1:2, :] + jnp.sum(q * k, axis=-1, keepdims=True) * v_new
    r = lax.broadcasted_iota(jnp.int32, (HEAD_DIM, HEAD_DIM), 0)
    c = lax.broadcasted_iota(jnp.int32, (HEAD_DIM, HEAD_DIM), 1)
    kcol = jnp.sum(jnp.where(r == c, jnp.broadcast_to(k, (HEAD_DIM, HEAD_DIM)), 0.0), axis=-1, keepdims=True)
    s_ref[...] = S * e + kcol * v_new
    o_ref[...] = _rms(o, ng_ref[...]) * _silu(gate_ref[...])


def _dn_step(qkv, gate, bg, S0, norm_g):
    B = qkv.shape[0]
    head = lambda off: pl.BlockSpec((None, 1, HEAD_DIM), lambda b, h: (b, 0, off + h))
    st = pl.BlockSpec((None, None, HEAD_DIM, HEAD_DIM), lambda b, h: (b, h, 0, 0))
    return pl.pallas_call(
        _dn_step_kernel, grid=(B, N_HEADS),
        in_specs=[head(0), head(N_HEADS), head(2 * N_HEADS), head(0),
                  pl.BlockSpec((None, 1, LANE), lambda b, h: (b, 0, 0)), st,
                  pl.BlockSpec((1, HEAD_DIM), lambda b, h: (0, 0))],
        out_specs=(head(0), st),
        out_shape=(jax.ShapeDtypeStruct((B, 1, GROUP_WIDTH), F32),
                   jax.ShapeDtypeStruct((B, N_HEADS, HEAD_DIM, HEAD_DIM), F32)),
        compiler_params=_cparams("parallel", "parallel"),
    )(qkv, qkv, qkv, gate, bg, S0, norm_g.reshape(1, HEAD_DIM))


def _project_in(h2, w_main, w_tail):
    M = h2.shape[0]
    proj = _matmul(h2, w_main, tm=1024, tn=1024, n_out=IN_MAIN)
    pe = _matmul(h2, w_tail, tm=1024, tn=LANE)
    return proj, pe


def _state_to_rows(s):
    B = s.shape[0]
    return jnp.moveaxis(s, -1, 1).reshape(B, 2, -1)


def _rows_to_state(s):
    B = s.shape[0]
    return jnp.moveaxis(s.reshape(B, 2, S5_GROUPS, S5_STATE), 1, -1)


def _kv_out(proj, col, B, T):
    w = 2 * GROUP_WIDTH
    lo = col * LANE + GROUP_WIDTH
    return proj[:, lo:lo + w].reshape(B, T, 2, N_HEADS, HEAD_DIM)


def _layer(carry, lw, consts):
    xp, xs = carry
    (mod, cache_sb_l, cache_mb_l, st_s5, st_dn, st_dn_conv, st_ffn, layer) = lw['dyn']
    W = lw['w']
    cache_sb, cache_mb, page_table, slopes = consts
    Bp, Tp, D = xp.shape
    Bs = xs.shape[0]

    w_in = W['w_in'].astype(BF16)
    w_main = w_in
    w_tail = jnp.pad(w_in[:, IN_MAIN:], ((0, 0), (0, LANE - (w_in.shape[1] - IN_MAIN))))
    w_out = W['w_out'].astype(BF16)
    w_up = W['ffn_w_up'].astype(BF16)
    w_down = W['ffn_w_down'].astype(BF16)
    w_glu = W['s5_w_glu'].astype(BF16)
    F2 = w_up.shape[1]

    s5p = _s5_params(W['s5_lambda_re'], W['s5_lambda_im'], W['s5_log_dt'], W['s5_b_re'], W['s5_b_im'])
    packed = _s5_pack(*s5p, W['s5_c_re'], W['s5_c_im'])

    def mods(m):
        m = m.reshape(m.shape[0], N_MOD, 1, D)
        return [m[:, i] for i in range(N_MOD)]

    mp, ms = mods(mod[:Bp]), mods(mod[Bp:Bp + Bs])
    row = lambda a: a.reshape(1, -1)

    h = _prenorm(xp, row(W['g_pre_mix']), mp[1], mp[0])
    proj, pe = _project_in(h.reshape(Bp * Tp, D), w_main, w_tail)
    proj3, pe3 = proj.reshape(Bp, Tp, IN_MAIN), pe.reshape(Bp, Tp, LANE)
    y_s5, hs = _s5_scan(proj3, jnp.zeros((Bp, 2, S5_GROUPS * S5_STATE), F32), packed, W['s5_d'])
    y_s5 = _glu(y_s5.reshape(Bp * Tp, GROUP_WIDTH), w_glu, W['s5_b_glu'])
    y_sb = _sb_attention(proj3)
    y_mb = _moba_attention(proj3, slopes)
    K_dn = W['dn_conv_w'].shape[0]
    qkv_dn = _dn_prep(proj3, jnp.zeros((Bp, K_dn - 1, 3 * GROUP_WIDTH), F32), W['dn_conv_w'], W['dn_conv_b'])
    bd = _dn_gates(pe3, W['dn_a_log'], W['dn_dt_bias'])
    y_dn, dn_S = _dn_attention(qkv_dn, proj3, bd, jnp.zeros((Bp, N_HEADS, HEAD_DIM, HEAD_DIM), F32), W['dn_norm_g'])
    ycat = jnp.concatenate([y_s5.reshape(Bp, Tp, GROUP_WIDTH), y_sb, y_mb, y_dn], axis=-1)
    y = _matmul(ycat.reshape(Bp * Tp, 4 * GROUP_WIDTH), w_out, tm=1024, tn=1024)
    xp, h = _resid_norm(xp, y.reshape(Bp, Tp, D), mp[2], row(W['g_post_mix']), row(W['g_pre_ffn']), mp[4], mp[3])
    up = _matmul(h.reshape(Bp * Tp, D), w_up, tm=1024, tn=512).reshape(Bp, Tp, F2)
    K_f = W['ffn_conv_w'].shape[0]
    act = _ffn_gate(up, jnp.zeros((Bp, K_f - 1, F2), F32), W['ffn_conv_w'], W['ffn_conv_b'])
    y = _matmul(act.reshape(Bp * Tp, F2 // 2), w_down, tm=1024, tn=512, tk=F2 // 4)
    xp = _resid(xp, y.reshape(Bp, Tp, D), mp[5], row(W['g_post_ffn']))
    dn_lo = COL_DN * LANE
    out_p = (_kv_out(proj, COL_SB, Bp, Tp), _kv_out(proj, COL_MB, Bp, Tp), _rows_to_state(hs), dn_S,
             proj3[:, Tp - (K_dn - 1):, dn_lo:dn_lo + 3 * GROUP_WIDTH], up[:, Tp - (K_f - 1):, :])

    h = _prenorm(xs, row(W['g_pre_mix']), ms[1], ms[0])
    proj, pe = _project_in(h.reshape(Bs, D), w_main, w_tail)
    y_s5, hs = _s5_step(proj[:, :GROUP_WIDTH], _state_to_rows(st_s5), packed, W['s5_d'])
    y_s5 = _glu(y_s5, w_glu, W['s5_b_glu'])
    heads = lambda col: proj[:, col * LANE:col * LANE + GROUP_WIDTH].reshape(Bs, N_HEADS, HEAD_DIM)
    y_sb = _sb_step(heads(COL_SB), cache_sb, page_table, layer)
    y_mb = _moba_step(heads(COL_MB), heads(COL_MB + N_HEADS), heads(COL_MB + 2 * N_HEADS),
                      cache_mb, page_table, layer, slopes)
    x_dn = proj[:, dn_lo:dn_lo + 3 * GROUP_WIDTH]
    qkv_dn, bg = _dn_step_prep(x_dn, st_dn_conv, W['dn_conv_w'], W['dn_conv_b'], pe, W['dn_a_log'], W['dn_dt_bias'])
    gate_dn = proj[:, COL_DG * LANE:COL_DG * LANE + GROUP_WIDTH]
    y_dn, dn_S = _dn_step(qkv_dn.reshape(Bs, 1, -1), gate_dn.reshape(Bs, 1, -1), bg.reshape(Bs, 1, LANE),
                          st_dn, W['dn_norm_g'])
    ycat = jnp.concatenate([y_s5, y_sb.reshape(Bs, -1).astype(BF16), y_mb.reshape(Bs, -1).astype(BF16),
                            y_dn.reshape(Bs, -1).astype(BF16)], axis=-1)
    y = _matmul(ycat, w_out, tm=Bs, tn=1024)
    xs, h = _resid_norm(xs, y.reshape(Bs, 1, D), ms[2], row(W['g_post_mix']), row(W['g_pre_ffn']), ms[4], ms[3])
    up = _matmul(h.reshape(Bs, D), w_up, tm=Bs, tn=512)
    act = _ffn_gate_step(up, st_ffn, W['ffn_conv_w'], W['ffn_conv_b'])
    y = _matmul(act, w_down, tm=Bs, tn=512, tk=F2 // 4)
    xs = _resid(xs, y.reshape(Bs, 1, D), ms[5], row(W['g_post_ffn']))
    out_s = (_kv_out(proj, COL_SB, Bs, 1), _kv_out(proj, COL_MB, Bs, 1), _rows_to_state(hs), dn_S,
             jnp.concatenate([st_dn_conv[:, 1:], x_dn[:, None, :]], axis=1),
             jnp.concatenate([st_ffn[:, 1:], up[:, None, :]], axis=1))
    return (xp, xs), (out_p, out_s)


def kernel(x_prompt, x_sample, cache_sb_kv, cache_moba_kv, state_s5, state_dn, state_dn_conv, state_ffn_conv, page_table, c_prompt, c_sample, w_in, w_out, w_ada, b_ada, g_pre_mix, g_post_mix, g_pre_ffn, g_post_ffn, s5_lambda_re, s5_lambda_im, s5_log_dt, s5_b_re, s5_b_im, s5_c_re, s5_c_im, s5_d, s5_w_glu, s5_b_glu, dn_conv_w, dn_conv_b, dn_a_log, dn_dt_bias, dn_norm_g, ffn_w_up, ffn_conv_w, ffn_conv_b, ffn_w_down):
    depth = w_in.shape[0]
    Bp, Bs = x_prompt.shape[0], x_sample.shape[0]
    rows = -(-(Bp + Bs) // SUBLANE) * SUBLANE
    c_all = jnp.concatenate([c_prompt, c_sample, jnp.zeros((rows - Bp - Bs, c_prompt.shape[1]), F32)], axis=0)
    mod = _ada_all(c_all, w_ada, b_ada)
    slopes = 2.0 ** (-8.0 * (jnp.arange(N_HEADS, dtype=F32) + 1.0) / N_HEADS)
    weights = dict(w_in=w_in, w_out=w_out, g_pre_mix=g_pre_mix, g_post_mix=g_post_mix, g_pre_ffn=g_pre_ffn,
                   g_post_ffn=g_post_ffn, s5_lambda_re=s5_lambda_re, s5_lambda_im=s5_lambda_im,
                   s5_log_dt=s5_log_dt, s5_b_re=s5_b_re, s5_b_im=s5_b_im, s5_c_re=s5_c_re, s5_c_im=s5_c_im,
                   s5_d=s5_d, s5_w_glu=s5_w_glu, s5_b_glu=s5_b_glu, dn_conv_w=dn_conv_w, dn_conv_b=dn_conv_b,
                   dn_a_log=dn_a_log, dn_dt_bias=dn_dt_bias, dn_norm_g=dn_norm_g, ffn_w_up=ffn_w_up,
                   ffn_conv_w=ffn_conv_w, ffn_conv_b=ffn_conv_b, ffn_w_down=ffn_w_down)
    dyn = (mod, None, None, state_s5, state_dn, state_dn_conv, state_ffn_conv, jnp.arange(depth, dtype=jnp.int32))
    consts = (cache_sb_kv, cache_moba_kv, page_table, slopes)
    step = lambda carry, lw: _layer(carry, lw, consts)
    (xp, xs), (out_p, out_s) = lax.scan(step, (x_prompt, x_sample), dict(dyn=dyn, w=weights))
    outs = [xp, xs]
    for p, s in zip(out_p, out_s):
        outs += [p, s]
    return tuple(outs)
```

```python
import functools

import jax
import jax.numpy as jnp
from jax import lax
from jax.experimental import pallas as pl
from jax.experimental.pallas import tpu as pltpu

F32 = jnp.float32
BF16 = jnp.bfloat16

HEAD_DIM = 128
N_HEADS = 8
GROUP_WIDTH = N_HEADS * HEAD_DIM
S5_CH = 16
S5_STATE = 64
S5_GROUPS = GROUP_WIDTH // S5_CH
S5_GT = 8
S5_NT = S5_GROUPS // S5_GT
S5_TS = S5_GT * S5_STATE
MOBA_BLOCK = 256
MOBA_TOPK = 3
DN_CHUNK = 64
N_MOD = 6
EPS = 1e-6
ATT_SCALE = HEAD_DIM ** -0.5
NEG = -1e30

LANE = 128
SUBLANE = 8
VMEM_LIMIT_BYTES = 56 * 2 ** 20

COL_S5 = 0
COL_SB = GROUP_WIDTH // LANE
COL_MB = 4 * GROUP_WIDTH // LANE
COL_DN = 7 * GROUP_WIDTH // LANE
COL_DG = 10 * GROUP_WIDTH // LANE
IN_MAIN = 11 * GROUP_WIDTH


def _cparams(*sem):
    return pltpu.CompilerParams(dimension_semantics=sem, vmem_limit_bytes=VMEM_LIMIT_BYTES)


def _dot(a, b):
    return jnp.dot(a.astype(BF16), b.astype(BF16), preferred_element_type=F32)


def _dot_nt(a, b):
    return lax.dot_general(a.astype(BF16), b.astype(BF16), (((1,), (1,)), ((), ())),
                           preferred_element_type=F32)


def _split(x):
    hi = x.astype(BF16)
    lo = (x - hi.astype(F32)).astype(BF16)
    return hi, lo


def _dot_x01(x, m01):
    hi, lo = _split(x)
    return (jnp.dot(hi, m01, preferred_element_type=F32)
            + jnp.dot(lo, m01, preferred_element_type=F32))


def _softplus(z):
    return jnp.maximum(z, 0.0) + jnp.log1p(jnp.exp(-jnp.abs(z)))


def _silu(x):
    return x * jax.nn.sigmoid(x)


def _shift_rows(x, d):
    rows = lax.broadcasted_iota(jnp.int32, x.shape, 0)
    return jnp.where(rows >= d, pltpu.roll(x, d, 0), 0.0)


def _mm_kernel(x_ref, w_ref, o_ref):
    o_ref[...] = _dot(x_ref[...], w_ref[...]).astype(o_ref.dtype)


def _mm_acc_kernel(x_ref, w_ref, o_ref, acc_ref):
    k = pl.program_id(2)
    part = _dot(x_ref[...], w_ref[...])

    @pl.when(k == 0)
    def _():
        acc_ref[...] = part

    @pl.when(k > 0)
    def _():
        acc_ref[...] += part

    @pl.when(k == pl.num_programs(2) - 1)
    def _():
        o_ref[...] = acc_ref[...].astype(o_ref.dtype)


def _mm_kernel_l(l_ref, x_ref, w_ref, o_ref):
    _mm_kernel(x_ref, w_ref, o_ref)


def _mm_acc_kernel_l(l_ref, x_ref, w_ref, o_ref, acc_ref):
    _mm_acc_kernel(x_ref, w_ref, o_ref, acc_ref)


def _matmul(x, w, layer, *, tm, tn, tk=None, n_out=None, out_dtype=F32):
    M, K = x.shape
    n_out = w.shape[2] if n_out is None else n_out
    tm, tn = min(tm, M), min(tn, n_out)
    tk = K if tk is None else tk
    assert M % tm == 0 and n_out % tn == 0 and K % tk == 0
    nk = K // tk
    lidx = jnp.reshape(layer, (1,)).astype(jnp.int32)
    if nk == 1:
        return pl.pallas_call(
            _mm_kernel_l,
            grid_spec=pltpu.PrefetchScalarGridSpec(
                num_scalar_prefetch=1, grid=(M // tm, n_out // tn),
                in_specs=[pl.BlockSpec((tm, K), lambda i, j, l: (i, 0)),
                          pl.BlockSpec((None, K, tn), lambda i, j, l: (l[0], 0, j))],
                out_specs=pl.BlockSpec((tm, tn), lambda i, j, l: (i, j))),
            out_shape=jax.ShapeDtypeStruct((M, n_out), out_dtype),
            compiler_params=_cparams("parallel", "parallel"),
        )(lidx, x, w)
    return pl.pallas_call(
        _mm_acc_kernel_l,
        grid_spec=pltpu.PrefetchScalarGridSpec(
            num_scalar_prefetch=1, grid=(M // tm, n_out // tn, nk),
            in_specs=[pl.BlockSpec((tm, tk), lambda i, j, k, l: (i, k)),
                      pl.BlockSpec((None, tk, tn), lambda i, j, k, l: (l[0], k, j))],
            out_specs=pl.BlockSpec((tm, tn), lambda i, j, k, l: (i, j)),
            scratch_shapes=[pltpu.VMEM((tm, tn), F32)]),
        out_shape=jax.ShapeDtypeStruct((M, n_out), out_dtype),
        compiler_params=_cparams("parallel", "parallel", "arbitrary"),
    )(lidx, x, w)


def _ada_kernel(c_ref, w_ref, b_ref, o_ref):
    o_ref[...] = _dot(_silu(c_ref[...]), w_ref[...]) + b_ref[...]


def _ada_all(c, w_ada, b_ada, *, tn=512):
    R, D = c.shape
    L, _, N = w_ada.shape
    tn = min(tn, N)
    return pl.pallas_call(
        _ada_kernel,
        grid=(L, N // tn),
        in_specs=[pl.BlockSpec((R, D), lambda l, j: (0, 0)),
                  pl.BlockSpec((None, D, tn), lambda l, j: (l, 0, j)),
                  pl.BlockSpec((None, 1, tn), lambda l, j: (l, 0, j))],
        out_specs=pl.BlockSpec((None, R, tn), lambda l, j: (l, 0, j)),
        out_shape=jax.ShapeDtypeStruct((L, R, N), F32),
        compiler_params=_cparams("parallel", "parallel"),
    )(c, w_ada, b_ada.reshape(L, 1, N))


def _rms(x, g):
    return x * lax.rsqrt(jnp.mean(x * x, axis=-1, keepdims=True) + EPS) * g


def _prenorm_kernel(x_ref, g_ref, sc_ref, sh_ref, h_ref):
    h = _rms(x_ref[...], g_ref[...]) * (1.0 + sc_ref[...]) + sh_ref[...]
    h_ref[...] = h.astype(h_ref.dtype)


def _resid_norm_kernel(x_ref, y_ref, gate_ref, gpost_ref, gnext_ref, sc_ref, sh_ref, xo_ref, h_ref):
    x = x_ref[...] + gate_ref[...] * _rms(y_ref[...], gpost_ref[...])
    xo_ref[...] = x
    h = _rms(x, gnext_ref[...]) * (1.0 + sc_ref[...]) + sh_ref[...]
    h_ref[...] = h.astype(h_ref.dtype)


def _resid_kernel(x_ref, y_ref, gate_ref, gpost_ref, xo_ref):
    xo_ref[...] = x_ref[...] + gate_ref[...] * _rms(y_ref[...], gpost_ref[...])


def _row_specs(T, D, tr):
    act = pl.BlockSpec((None, tr, D), lambda b, i: (b, i, 0))
    per_seq = pl.BlockSpec((None, 1, D), lambda b, i: (b, 0, 0))
    gain = pl.BlockSpec((1, D), lambda b, i: (0, 0))
    return act, per_seq, gain


def _prenorm(x, g, scale, shift, *, tr=256):
    B, T, D = x.shape
    tr = min(tr, T)
    act, per_seq, gain = _row_specs(T, D, tr)
    return pl.pallas_call(
        _prenorm_kernel, grid=(B, T // tr),
        in_specs=[act, gain, per_seq, per_seq], out_specs=act,
        out_shape=jax.ShapeDtypeStruct((B, T, D), BF16),
        compiler_params=_cparams("parallel", "parallel"),
    )(x, g, scale, shift)


def _resid_norm(x, y, gate, g_post, g_next, scale, shift, *, tr=256):
    B, T, D = x.shape
    tr = min(tr, T)
    act, per_seq, gain = _row_specs(T, D, tr)
    return pl.pallas_call(
        _resid_norm_kernel, grid=(B, T // tr),
        in_specs=[act, act, per_seq, gain, gain, per_seq, per_seq], out_specs=(act, act),
        out_shape=(jax.ShapeDtypeStruct((B, T, D), F32), jax.ShapeDtypeStruct((B, T, D), BF16)),
        compiler_params=_cparams("parallel", "parallel"),
    )(x, y, gate, g_post, g_next, scale, shift)


def _resid(x, y, gate, g_post, *, tr=256):
    B, T, D = x.shape
    tr = min(tr, T)
    act, per_seq, gain = _row_specs(T, D, tr)
    return pl.pallas_call(
        _resid_kernel, grid=(B, T // tr),
        in_specs=[act, act, per_seq, gain], out_specs=act,
        out_shape=jax.ShapeDtypeStruct((B, T, D), F32),
        compiler_params=_cparams("parallel", "parallel"),
    )(x, y, gate, g_post)


def _conv_rows(x, prev_rows, w_ref, b_ref):
    K = w_ref.shape[0]
    rows = lax.broadcasted_iota(jnp.int32, x.shape, 0)
    y = b_ref[...] + x * w_ref[K - 1:K, :]
    for d in range(1, K):
        xs = pltpu.roll(x, d, 0)
        for r in range(d):
            xs = jnp.where(rows == r, prev_rows[d - r - 1], xs)
        y = y + xs * w_ref[K - 1 - d:K - d, :]
    return y


def _prev_rows(first, state_ref, tail_ref, n):
    ns, nt = state_ref.shape[0], tail_ref.shape[0]
    return [jnp.where(first, state_ref[ns - 1 - k:ns - k, :], tail_ref[nt - 1 - k:nt - k, :])
            for k in range(n)]


def _ffn_gate_kernel(a_ref, g_ref, ta_ref, tg_ref, sa_ref, sg_ref, wa_ref, wg_ref, ba_ref, bg_ref, o_ref):
    first = pl.program_id(1) == 0
    K = wa_ref.shape[0]
    a = _conv_rows(a_ref[...], _prev_rows(first, sa_ref, ta_ref, K - 1), wa_ref, ba_ref)
    g = _conv_rows(g_ref[...], _prev_rows(first, sg_ref, tg_ref, K - 1), wg_ref, bg_ref)
    o_ref[...] = (_silu(g) * a).astype(o_ref.dtype)


def _ffn_gate(up, prev, conv_w, conv_b, *, tt=128, tc=5504):
    B, T, F2 = up.shape
    F = F2 // 2
    K = conv_w.shape[0]
    tt, tc = min(tt, T), min(tc, F)
    assert T % tt == 0 and F % tc == 0 and tt % SUBLANE == 0
    nc = F // tc
    tb = tt // SUBLANE
    cur = lambda off: pl.BlockSpec((None, tt, tc), lambda b, i, j: (b, i, j + off))
    tail = lambda off: pl.BlockSpec((None, SUBLANE, tc), lambda b, i, j: (b, jnp.maximum(i * tb - 1, 0), j + off))
    st = lambda off: pl.BlockSpec((None, K - 1, tc), lambda b, i, j: (b, 0, j + off))
    wsp = lambda off: pl.BlockSpec((K, tc), lambda b, i, j: (0, j + off))
    bsp = lambda off: pl.BlockSpec((1, tc), lambda b, i, j: (0, j + off))
    return pl.pallas_call(
        _ffn_gate_kernel, grid=(B, T // tt, nc),
        in_specs=[cur(0), cur(nc), tail(0), tail(nc), st(0), st(nc), wsp(0), wsp(nc), bsp(0), bsp(nc)],
        out_specs=pl.BlockSpec((None, tt, tc), lambda b, i, j: (b, i, j)),
        out_shape=jax.ShapeDtypeStruct((B, T, F), BF16),
        compiler_params=_cparams("parallel", "parallel", "parallel"),
    )(up, up, up, up, prev, prev, conv_w, conv_w, conv_b.reshape(1, F2), conv_b.reshape(1, F2))


def _ffn_gate_step_kernel(up_ref, prev_ref, w_ref, b_ref, o_ref):
    F = o_ref.shape[1]
    K = w_ref.shape[0]
    y = b_ref[...] + up_ref[...] * w_ref[K - 1:K, :]
    for k in range(K - 1):
        y = y + prev_ref[:, k, :] * w_ref[k:k + 1, :]
    o_ref[...] = (_silu(y[:, F:]) * y[:, :F]).astype(o_ref.dtype)


def _ffn_gate_step(up, prev, conv_w, conv_b):
    B, F2 = up.shape
    return pl.pallas_call(
        _ffn_gate_step_kernel,
        out_shape=jax.ShapeDtypeStruct((B, F2 // 2), BF16),
        compiler_params=pltpu.CompilerParams(vmem_limit_bytes=VMEM_LIMIT_BYTES),
    )(up, prev, conv_w, conv_b.reshape(1, F2))


def _s5_param_kernel(lr_ref, li_ref, dt_ref, br_ref, bi_ref, ar_ref, ai_ref, bbr_ref, bbi_ref):
    lr, li = lr_ref[...], li_ref[...]
    dt = jnp.exp(dt_ref[...])
    mag = jnp.exp(lr * dt)
    ar, ai = mag * jnp.cos(li * dt), mag * jnp.sin(li * dt)
    den = lr * lr + li * li
    fr = ((ar - 1.0) * lr + ai * li) / den
    fi = (ai * lr - (ar - 1.0) * li) / den
    ar_ref[...], ai_ref[...] = ar, ai
    bbr_ref[...] = fr * br_ref[...] - fi * bi_ref[...]
    bbi_ref[...] = fr * bi_ref[...] + fi * br_ref[...]


def _s5_params(lam_re, lam_im, log_dt, b_re, b_im):
    G, P = lam_re.shape
    CH = b_re.shape[-1]
    col = lambda a: a.reshape(G * P, 1)
    dt = jnp.broadcast_to(log_dt[:, None], (G, P))
    return pl.pallas_call(
        _s5_param_kernel,
        out_shape=(jax.ShapeDtypeStruct((G * P, 1), F32),) * 2 + (jax.ShapeDtypeStruct((G * P, CH), F32),) * 2,
    )(col(lam_re), col(lam_im), col(dt), b_re.reshape(G * P, CH), b_im.reshape(G * P, CH))


def _s5_pack(ar, ai, bbr, bbi, c_re, c_im):
    eye = jnp.eye(S5_GT, dtype=F32)
    a_re = ar.reshape(S5_NT, 1, S5_TS)
    a_im = ai.reshape(S5_NT, 1, S5_TS)

    def b_dense(bb):
        bb = bb.reshape(S5_NT, S5_GT, S5_STATE, S5_CH)
        return jnp.einsum('tgpc,gh->tgchp', bb, eye).reshape(S5_NT, S5_GT * S5_CH, S5_TS)

    def c_dense(c):
        c = c.reshape(S5_NT, S5_GT, S5_CH, S5_STATE)
        return jnp.einsum('tgcp,gh->tgphc', c, eye).reshape(S5_NT, S5_TS, S5_GT * S5_CH)

    bd = jnp.concatenate([b_dense(bbr), b_dense(bbi)], axis=-1).astype(BF16)
    return a_re, a_im, bd, c_dense(c_re).astype(BF16), c_dense(c_im).astype(BF16)


def _cmul(ar, ai, br, bi):
    return ar * br - ai * bi, ar * bi + ai * br


def _s5_kernel(u_ref, h0_ref, are_ref, aim_ref, bd_ref, cre_ref, cim_ref, d_ref, y_ref, hs_ref,
               pr_ref, pi_ref, carry_ref, h_ref):
    c = pl.program_id(2)
    L = u_ref.shape[0]
    G = pr_ref.shape[0]

    @pl.when(c == 0)
    def _():
        pr_ref[0:1, :] = are_ref[...]
        pi_ref[0:1, :] = aim_ref[...]
        n = 1
        while n < G:
            tr, ti = pr_ref[n - 1:n, :], pi_ref[n - 1:n, :]
            qr, qi = _cmul(pr_ref[0:n, :], pi_ref[0:n, :], tr, ti)
            pr_ref[n:2 * n, :] = qr
            pi_ref[n:2 * n, :] = qi
            n *= 2
        carry_ref[...] = h0_ref[...]

    u = u_ref[...]
    h_ref[...] = _dot(u, bd_ref[...])

    def group(t, carry):
        r0 = pl.multiple_of(t * G, G)
        hr, hi = h_ref[pl.ds(r0, G), :S5_TS], h_ref[pl.ds(r0, G), S5_TS:]
        d = 1
        while d < G:
            sr, si = _cmul(pr_ref[d - 1:d, :], pi_ref[d - 1:d, :], _shift_rows(hr, d), _shift_rows(hi, d))
            hr, hi = hr + sr, hi + si
            d *= 2
        cr, ci = _cmul(pr_ref[...], pi_ref[...], carry[0], carry[1])
        hr, hi = hr + cr, hi + ci
        h_ref[pl.ds(r0, G), :S5_TS] = hr
        h_ref[pl.ds(r0, G), S5_TS:] = hi
        return hr[G - 1:G, :], hi[G - 1:G, :]

    cr, ci = lax.fori_loop(0, L // G, group, (carry_ref[0:1, :], carry_ref[1:2, :]), unroll=4)
    carry_ref[0:1, :] = cr
    carry_ref[1:2, :] = ci
    y = _dot(h_ref[:, :S5_TS], cre_ref[...]) - _dot(h_ref[:, S5_TS:], cim_ref[...]) + u * d_ref[...]
    y_ref[...] = jax.nn.gelu(y)

    @pl.when(c == pl.num_programs(2) - 1)
    def _():
        hs_ref[...] = carry_ref[...]


def _s5_scan(proj, h0, packed, d_skip, *, L=256):
    B, T, _ = proj.shape
    a_re, a_im, bd, cre, cim = packed
    L = min(L, T)
    G = min(SUBLANE, L)
    assert T % L == 0 and L % G == 0 and G & (G - 1) == 0
    tile = lambda t: pl.BlockSpec((None,) + t, lambda b, g, c: (g, 0, 0))
    return pl.pallas_call(
        _s5_kernel, grid=(B, S5_NT, T // L),
        in_specs=[pl.BlockSpec((None, L, LANE), lambda b, g, c: (b, c, COL_S5 + g)),
                  pl.BlockSpec((None, 2, S5_TS), lambda b, g, c: (b, 0, g)),
                  tile((1, S5_TS)), tile((1, S5_TS)), tile((LANE, 2 * S5_TS)),
                  tile((S5_TS, LANE)), tile((S5_TS, LANE)),
                  pl.BlockSpec((1, LANE), lambda b, g, c: (0, g))],
        out_specs=(pl.BlockSpec((None, L, LANE), lambda b, g, c: (b, c, g)),
                   pl.BlockSpec((None, 2, S5_TS), lambda b, g, c: (b, 0, g))),
        out_shape=(jax.ShapeDtypeStruct((B, T, GROUP_WIDTH), F32),
                   jax.ShapeDtypeStruct((B, 2, S5_GROUPS * S5_STATE), F32)),
        scratch_shapes=[pltpu.VMEM((G, S5_TS), F32), pltpu.VMEM((G, S5_TS), F32),
                        pltpu.VMEM((2, S5_TS), F32), pltpu.VMEM((L, 2 * S5_TS), F32)],
        compiler_params=_cparams("parallel", "parallel", "arbitrary"),
    )(proj, h0, a_re, a_im, bd, cre, cim, d_skip.reshape(1, GROUP_WIDTH))


def _s5_step_kernel(u_ref, h0_ref, are_ref, aim_ref, bd_ref, cre_ref, cim_ref, d_ref, y_ref, hs_ref):
    for g in range(S5_NT):
        u = u_ref[:, g * LANE:(g + 1) * LANE]
        bu = _dot(u, bd_ref[g])
        sl = slice(g * S5_TS, (g + 1) * S5_TS)
        ar, ai = are_ref[g], aim_ref[g]
        pr, pi = _cmul(ar, ai, h0_ref[:, 0, sl], h0_ref[:, 1, sl])
        hr, hi = bu[:, :S5_TS] + pr, bu[:, S5_TS:] + pi
        hs_ref[:, 0, sl] = hr
        hs_ref[:, 1, sl] = hi
        y = _dot(hr, cre_ref[g]) - _dot(hi, cim_ref[g]) + u * d_ref[:, g * LANE:(g + 1) * LANE]
        y_ref[:, g * LANE:(g + 1) * LANE] = jax.nn.gelu(y)


def _s5_step(u, h0, packed, d_skip):
    B = u.shape[0]
    a_re, a_im, bd, cre, cim = packed
    return pl.pallas_call(
        _s5_step_kernel,
        out_shape=(jax.ShapeDtypeStruct((B, GROUP_WIDTH), F32),
                   jax.ShapeDtypeStruct((B, 2, S5_GROUPS * S5_STATE), F32)),
        compiler_params=pltpu.CompilerParams(vmem_limit_bytes=VMEM_LIMIT_BYTES),
    )(u, h0, a_re, a_im, bd, cre, cim, d_skip.reshape(1, GROUP_WIDTH))


def _glu_kernel(y_ref, w_ref, b_ref, o_ref):
    y = y_ref[...]
    o_ref[...] = (y * jax.nn.sigmoid(_dot(y, w_ref[...]) + b_ref[...])).astype(o_ref.dtype)


def _glu(y, w, b, *, tm=1024):
    M, N = y.shape
    tm = min(tm, M)
    return pl.pallas_call(
        _glu_kernel, grid=(M // tm,),
        in_specs=[pl.BlockSpec((tm, N), lambda i: (i, 0)),
                  pl.BlockSpec((N, N), lambda i: (0, 0)),
                  pl.BlockSpec((1, N), lambda i: (0, 0))],
        out_specs=pl.BlockSpec((tm, N), lambda i: (i, 0)),
        out_shape=jax.ShapeDtypeStruct((M, N), BF16),
        compiler_params=_cparams("parallel"),
    )(y, w, b.reshape(1, N))


def _sb_block(q, kb, vb, upper, run, acc, mask):
    z = _dot_nt(q, kb) * ATT_SCALE
    sp = _softplus(z)
    log_keep = -sp if mask is None else jnp.where(mask, -sp, 0.0)
    later = _dot_x01(log_keep, upper)
    w = jnp.exp(z - sp + later + run)
    if mask is not None:
        w = jnp.where(mask, w, 0.0)
    acc = acc + _dot(w, vb)
    run = run + later[:, 0:1] + log_keep[:, 0:1]
    return run, acc


def _sb_kernel(q_ref, k_ref, v_ref, o_ref, *, tk):
    i = pl.program_id(2)
    tq = q_ref.shape[0]
    nd = tq // tk
    q = q_ref[...]
    ur = lax.broadcasted_iota(jnp.int32, (tk, tk), 0)
    uc = lax.broadcasted_iota(jnp.int32, (tk, tk), 1)
    upper = jnp.where(ur > uc, 1.0, 0.0).astype(BF16)
    r = lax.broadcasted_iota(jnp.int32, (tq, tk), 0)
    c = lax.broadcasted_iota(jnp.int32, (tq, tk), 1)
    run, acc = jnp.zeros((tq, 1), F32), jnp.zeros((tq, HEAD_DIM), F32)
    for d in reversed(range(nd)):
        start = pl.multiple_of(i * tq + d * tk, tk)
        run, acc = _sb_block(q, k_ref[pl.ds(start, tk), :], v_ref[pl.ds(start, tk), :], upper,
                             run, acc, c + d * tk < r)

    def body(jj, carry):
        start = pl.multiple_of((i * nd - 1 - jj) * tk, tk)
        return _sb_block(q, k_ref[pl.ds(start, tk), :], v_ref[pl.ds(start, tk), :], upper,
                         carry[0], carry[1], None)

    run, acc = lax.fori_loop(0, i * nd, body, (run, acc))
    o_ref[...] = acc.astype(o_ref.dtype)


def _sb_attention(proj, *, tq=512, tk=256):
    B, T, _ = proj.shape
    tq, tk = min(tq, T), min(tk, T)
    assert T % tq == 0 and tq % tk == 0
    seq = lambda off: pl.BlockSpec((None, T, HEAD_DIM), lambda b, h, i: (b, 0, off + h))
    return pl.pallas_call(
        functools.partial(_sb_kernel, tk=tk), grid=(B, N_HEADS, T // tq),
        in_specs=[pl.BlockSpec((None, tq, HEAD_DIM), lambda b, h, i: (b, i, COL_SB + h)),
                  seq(COL_SB + N_HEADS), seq(COL_SB + 2 * N_HEADS)],
        out_specs=pl.BlockSpec((None, tq, HEAD_DIM), lambda b, h, i: (b, i, h)),
        out_shape=jax.ShapeDtypeStruct((B, T, GROUP_WIDTH), BF16),
        compiler_params=_cparams("parallel", "parallel", "parallel"),
    )(proj, proj, proj)


def _head_layout(page_rows):
    cols = page_rows * N_HEADS
    r = lax.broadcasted_iota(jnp.int32, (N_HEADS, cols), 0)
    c = lax.broadcasted_iota(jnp.int32, (N_HEADS, cols), 1)
    own = (c % N_HEADS) == r
    tok = c // N_HEADS
    return own, tok


def _lane_chunks(x):
    return jnp.concatenate([x[:, j * LANE:(j + 1) * LANE] for j in range(x.shape[1] // LANE)], axis=0)


def _later_sums(x, upper):
    n = x.shape[1] // LANE
    within = _dot_x01(_lane_chunks(x), upper)
    out = [None] * n
    tail = jnp.zeros((N_HEADS, 1), F32)
    for j in reversed(range(n)):
        wj = within[j * N_HEADS:(j + 1) * N_HEADS, :]
        out[j] = wj + tail
        tail = tail + wj[:, 0:1] + x[:, j * LANE:j * LANE + 1]
    return jnp.concatenate(out, axis=1), tail


def _page_rows(kv_refs, which):
    P = kv_refs[0].shape[0]
    return jnp.concatenate([r[:, which, :, :].reshape(P * N_HEADS, HEAD_DIM) for r in kv_refs], axis=0)


def _sb_step_kernel(pt_ref, q_ref, *refs, g):
    kv_refs, (o_ref, run_ref, acc_ref) = refs[:g][::-1], refs[g:]
    s = pl.program_id(1)
    P = kv_refs[0].shape[0]

    @pl.when(s == 0)
    def _():
        run_ref[...] = jnp.zeros_like(run_ref)
        acc_ref[...] = jnp.zeros_like(acc_ref)

    own, _ = _head_layout(g * P)
    r = lax.broadcasted_iota(jnp.int32, (LANE, LANE), 0)
    c = lax.broadcasted_iota(jnp.int32, (LANE, LANE), 1)
    upper = jnp.where(r > c, 1.0, 0.0).astype(BF16)
    z = _dot_nt(q_ref[...], _page_rows(kv_refs, 0)) * ATT_SCALE
    sp = _softplus(z)
    log_keep = jnp.where(own, -sp, 0.0)
    later, total = _later_sums(log_keep, upper)
    run = run_ref[:, 0:1]
    w = jnp.where(own, jnp.exp(z - sp + later + run), 0.0)
    acc_ref[...] += _dot(w, _page_rows(kv_refs, 1))
    run_ref[...] = jnp.broadcast_to(run + total, run_ref.shape)

    @pl.when(s == pl.num_programs(1) - 1)
    def _():
        o_ref[...] = acc_ref[...]


STEP_PAGES = 2


def _sb_step(q, cache, page_table, layer):
    B = q.shape[0]
    n_pages = page_table.shape[1]
    P = cache.shape[2]
    g = STEP_PAGES
    assert n_pages % g == 0
    pt = jnp.concatenate([page_table.reshape(-1), jnp.reshape(layer, (1,)).astype(jnp.int32)])

    def page(t):
        return pl.BlockSpec((None, None, P, 2, N_HEADS, HEAD_DIM),
                            lambda b, s, pt: (pt[B * n_pages], pt[b * n_pages + n_pages - 1 - (s * g + t)], 0, 0, 0, 0))

    return pl.pallas_call(
        functools.partial(_sb_step_kernel, g=g),
        grid_spec=pltpu.PrefetchScalarGridSpec(
            num_scalar_prefetch=1, grid=(B, n_pages // g),
            in_specs=[pl.BlockSpec((None, N_HEADS, HEAD_DIM), lambda b, s, pt: (b, 0, 0))] + [page(t) for t in range(g)],
            out_specs=pl.BlockSpec((None, N_HEADS, HEAD_DIM), lambda b, s, pt: (b, 0, 0)),
            scratch_shapes=[pltpu.VMEM((N_HEADS, LANE), F32), pltpu.VMEM((N_HEADS, HEAD_DIM), F32)]),
        out_shape=jax.ShapeDtypeStruct((B, N_HEADS, HEAD_DIM), F32),
        compiler_params=_cparams("parallel", "arbitrary"),
    )(pt, q, *([cache] * g))


def _dot_nt_precise(a, b):
    ah, al = _split(a)
    bh, bl = _split(b)
    return _dot_nt(ah, bh) + _dot_nt(ah, bl) + _dot_nt(al, bh)


def _topk_mask(gate, n_valid, k):
    n = gate.shape[0]
    idx = lax.broadcasted_iota(jnp.int32, gate.shape, 0)
    rank = jnp.zeros(gate.shape, F32)
    for j in range(n):
        gj = gate[j:j + 1, :]
        ahead = (gj > gate) | ((gj == gate) & (j < idx))
        rank = rank + jnp.where(ahead, jnp.where(j < n_valid, 1.0, 0.0), 0.0)
    return jnp.where(idx < n_valid, jnp.where(rank < k, 1.0, 0.0), 0.0)


def _moba_kernel(slope_ref, q_ref, k_ref, v_ref, o_ref, sel_ref):
    h, i = pl.program_id(1), pl.program_id(2)
    bs = q_ref.shape[0]
    nb = k_ref.shape[0] // bs
    slope = slope_ref[h]
    q = q_ref[...]
    kmean = jnp.concatenate(
        [jnp.mean(k_ref[j * bs:(j + 1) * bs, :], axis=0, keepdims=True) for j in range(nb)], axis=0)
    r = lax.broadcasted_iota(jnp.int32, (bs, bs), 0)
    c = lax.broadcasted_iota(jnp.int32, (bs, bs), 1)
    gate = _dot_nt_precise(kmean, q)
    sel_t = _topk_mask(gate, i, MOBA_TOPK)
    sel = _dot_nt(jnp.where(r == c, 1.0, 0.0), sel_t)
    for j in range(nb):
        sel_ref[j] = sel[:, j:j + 1]

    rel = (r - c).astype(F32)
    own = pl.multiple_of(i * bs, bs)
    s = _dot_nt(q, k_ref[pl.ds(own, bs), :]) * ATT_SCALE - slope * rel
    s = jnp.where(c <= r, s, NEG)
    m = jnp.max(s, axis=-1, keepdims=True)
    p = jnp.exp(s - m)
    l = jnp.sum(p, axis=-1, keepdims=True)
    acc = _dot(p, v_ref[pl.ds(own, bs), :])

    def body(j, carry):
        m, l, acc = carry
        start = pl.multiple_of(j * bs, bs)
        dist = rel + ((i - j) * bs).astype(F32)
        s = _dot_nt(q, k_ref[pl.ds(start, bs), :]) * ATT_SCALE - slope * dist
        s = jnp.where(sel_ref[j] > 0.0, s, NEG)
        m_new = jnp.maximum(m, jnp.max(s, axis=-1, keepdims=True))
        a = jnp.exp(m - m_new)
        p = jnp.exp(s - m_new)
        return m_new, a * l + jnp.sum(p, axis=-1, keepdims=True), a * acc + _dot(p, v_ref[pl.ds(start, bs), :])

    m, l, acc = lax.fori_loop(0, i, body, (m, l, acc))
    o_ref[...] = (acc / l).astype(o_ref.dtype)


def _moba_attention(proj, slopes):
    B, T, _ = proj.shape
    bs = min(MOBA_BLOCK, T)
    assert T % bs == 0
    nb = T // bs
    seq = lambda off: pl.BlockSpec((None, T, HEAD_DIM), lambda b, h, i, sl: (b, 0, off + h))
    return pl.pallas_call(
        _moba_kernel,
        grid_spec=pltpu.PrefetchScalarGridSpec(
            num_scalar_prefetch=1, grid=(B, N_HEADS, nb),
            in_specs=[pl.BlockSpec((None, bs, HEAD_DIM), lambda b, h, i, sl: (b, i, COL_MB + h)),
                      seq(COL_MB + N_HEADS), seq(COL_MB + 2 * N_HEADS)],
            out_specs=pl.BlockSpec((None, bs, HEAD_DIM), lambda b, h, i, sl: (b, i, h)),
            scratch_shapes=[pltpu.VMEM((nb, bs, 1), F32)]),
        out_shape=jax.ShapeDtypeStruct((B, T, GROUP_WIDTH), BF16),
        compiler_params=_cparams("parallel", "parallel", "parallel"),
    )(slopes, proj, proj, proj)


def _moba_step_kernel(pt_ref, q_ref, ks_ref, vs_ref, slope_ref, *refs, g):
    kv_refs, (o_ref, m_ref, l_ref, acc_ref, g_ref) = refs[:g], refs[g:]
    s = pl.program_id(1)
    nb = pl.num_programs(1)
    P = kv_refs[0].shape[0]
    own, tok = _head_layout(g * P)
    q = q_ref[...]
    slope = slope_ref[...]
    k2 = _page_rows(kv_refs, 0)
    dist = ((nb - s) * (g * P)).astype(F32) - tok.astype(F32)
    z = _dot_nt(q, k2) * ATT_SCALE - slope * dist
    z = jnp.where(own, z, NEG)
    m = jnp.max(z, axis=-1, keepdims=True)
    p = jnp.where(own, jnp.exp(z - m), 0.0)
    m_ref[s] = jnp.broadcast_to(m, (N_HEADS, LANE))
    l_ref[s] = jnp.broadcast_to(jnp.sum(p, axis=-1, keepdims=True), (N_HEADS, LANE))
    acc_ref[s] = _dot(p, _page_rows(kv_refs, 1))
    ksum = jnp.sum(k2.reshape(g * P, N_HEADS, HEAD_DIM), axis=0)
    g_ref[s] = jnp.broadcast_to(jnp.sum(q * ksum, axis=-1, keepdims=True), (N_HEADS, LANE)) * (1.0 / MOBA_BLOCK)

    @pl.when(s == nb - 1)
    def _():
        n = g_ref.shape[0]
        gate = [g_ref[j] for j in range(n)]
        sel = []
        for j in range(n):
            rank = jnp.zeros((N_HEADS, LANE), F32)
            for t in range(n):
                ahead = (gate[t] > gate[j]) if t > j else (gate[t] >= gate[j])
                if t != j:
                    rank = rank + jnp.where(ahead, 1.0, 0.0)
            sel.append(rank < MOBA_TOPK)
        s_self = jnp.sum(q * ks_ref[...], axis=-1, keepdims=True) * ATT_SCALE
        mx = jnp.broadcast_to(s_self, (N_HEADS, LANE))
        for j in range(n):
            mx = jnp.where(sel[j], jnp.maximum(mx, m_ref[j]), mx)
        e_self = jnp.exp(s_self - mx)
        den = e_self
        num = e_self * vs_ref[...]
        for j in range(n):
            wgt = jnp.where(sel[j], jnp.exp(m_ref[j] - mx), 0.0)
            den = den + wgt * l_ref[j]
            num = num + wgt * acc_ref[j]
        o_ref[...] = num / den


def _moba_step(q, k_self, v_self, cache, page_table, layer, slopes):
    B = q.shape[0]
    n_pages = page_table.shape[1]
    P = cache.shape[2]
    assert MOBA_BLOCK % P == 0 and (n_pages * P) % MOBA_BLOCK == 0
    g = MOBA_BLOCK // P
    nb = n_pages // g
    pt = jnp.concatenate([page_table.reshape(-1), jnp.reshape(layer, (1,)).astype(jnp.int32)])
    head = pl.BlockSpec((None, N_HEADS, HEAD_DIM), lambda b, s, pt: (b, 0, 0))
    per_block = pltpu.VMEM((nb, N_HEADS, LANE), F32)

    def page(t):
        return pl.BlockSpec((None, None, P, 2, N_HEADS, HEAD_DIM),
                            lambda b, s, pt: (pt[B * n_pages], pt[b * n_pages + s * g + t], 0, 0, 0, 0))

    return pl.pallas_call(
        functools.partial(_moba_step_kernel, g=g),
        grid_spec=pltpu.PrefetchScalarGridSpec(
            num_scalar_prefetch=1, grid=(B, nb),
            in_specs=[head, head, head, pl.BlockSpec((N_HEADS, 1), lambda b, s, pt: (0, 0))] + [page(t) for t in range(g)],
            out_specs=head,
            scratch_shapes=[per_block, per_block, per_block, per_block]),
        out_shape=jax.ShapeDtypeStruct((B, N_HEADS, HEAD_DIM), F32),
        compiler_params=_cparams("parallel", "arbitrary"),
    )(pt, q, k_self, v_self, slopes.reshape(N_HEADS, 1), *([cache] * g))


def _l2n(x):
    return x * lax.rsqrt(jnp.sum(x * x, axis=-1, keepdims=True) + EPS)


def _dn_prep_kernel(x_ref, tail_ref, st_ref, w_ref, b_ref, o_ref):
    part = pl.program_id(2)
    first = pl.program_id(1) == 0
    K = w_ref.shape[0]
    y = _silu(_conv_rows(x_ref[...], _prev_rows(first, st_ref, tail_ref, K - 1), w_ref, b_ref))

    @pl.when(part == 2)
    def _():
        o_ref[...] = y

    @pl.when(part < 2)
    def _():
        scale = jnp.where(part == 0, HEAD_DIM ** -0.5, 1.0)
        for hh in range(N_HEADS):
            sl = slice(hh * HEAD_DIM, (hh + 1) * HEAD_DIM)
            o_ref[:, sl] = _l2n(y[:, sl]) * scale


def _dn_prep(proj, prev, conv_w, conv_b, *, tt=256):
    B, T, _ = proj.shape
    K = conv_w.shape[0]
    GW = GROUP_WIDTH
    tt = min(tt, T)
    assert T % tt == 0 and tt % SUBLANE == 0
    tb = tt // SUBLANE
    cw = COL_DN * LANE // GW
    return pl.pallas_call(
        _dn_prep_kernel, grid=(B, T // tt, 3),
        in_specs=[pl.BlockSpec((None, tt, GW), lambda b, i, p: (b, i, cw + p)),
                  pl.BlockSpec((None, SUBLANE, GW), lambda b, i, p: (b, jnp.maximum(i * tb - 1, 0), cw + p)),
                  pl.BlockSpec((None, K - 1, GW), lambda b, i, p: (b, 0, p)),
                  pl.BlockSpec((K, GW), lambda b, i, p: (0, p)),
                  pl.BlockSpec((1, GW), lambda b, i, p: (0, p))],
        out_specs=pl.BlockSpec((None, tt, GW), lambda b, i, p: (b, i, p)),
        out_shape=jax.ShapeDtypeStruct((B, T, 3 * GW), F32),
        compiler_params=_cparams("parallel", "parallel", "parallel"),
    )(proj, proj, prev, conv_w, conv_b.reshape(1, 3 * GW))


def _dn_gate_values(pe, alog_ref, dtb_ref):
    col = lax.broadcasted_iota(jnp.int32, pe.shape, 1)
    g = -jnp.exp(alog_ref[...]) * _softplus(pe + dtb_ref[...])
    return col, jnp.where(col < N_HEADS, jax.nn.sigmoid(pe), g)


def _dn_gates_kernel(pe_ref, alog_ref, dtb_ref, o_ref):
    col, val = _dn_gate_values(pe_ref[...], alog_ref, dtb_ref)
    rows = lax.broadcasted_iota(jnp.int32, val.shape, 0)
    dec = jnp.where(col < N_HEADS, 0.0, val)
    d = 1
    while d < min(DN_CHUNK, val.shape[0]):
        dec = dec + jnp.where((rows % DN_CHUNK) >= d, pltpu.roll(dec, d, 0), 0.0)
        d *= 2
    o_ref[...] = jnp.where(col < N_HEADS, val, dec)


def _gate_rows(a_log, dt_bias):
    pad = lambda a: jnp.zeros((1, LANE), F32).at[0, N_HEADS:2 * N_HEADS].set(a)
    return pad(a_log), pad(dt_bias)


def _dn_gates(pe, a_log, dt_bias, *, tt=512):
    B, T, _ = pe.shape
    tt = min(tt, T)
    assert T % tt == 0 and (tt % DN_CHUNK == 0 or tt == T)
    alog, dtb = _gate_rows(a_log, dt_bias)
    row = pl.BlockSpec((1, LANE), lambda b, i: (0, 0))
    blk = pl.BlockSpec((None, tt, LANE), lambda b, i: (b, i, 0))
    return pl.pallas_call(
        _dn_gates_kernel, grid=(B, T // tt), in_specs=[blk, row, row], out_specs=blk,
        out_shape=jax.ShapeDtypeStruct((B, T, LANE), F32),
        compiler_params=_cparams("parallel", "parallel"),
    )(pe, alog, dtb)


def _pick_col(x, j):
    col = lax.broadcasted_iota(jnp.int32, x.shape, 1)
    return jnp.sum(jnp.where(col == j, x, 0.0), axis=-1, keepdims=True)


def _dot_precise(a, b):
    ah, al = _split(a)
    bh, bl = _split(b)
    return (jnp.dot(ah, bh, preferred_element_type=F32) + jnp.dot(ah, bl, preferred_element_type=F32)
            + jnp.dot(al, bh, preferred_element_type=F32))


SOLVE_BLOCK = 16
DN_SUPER = 4


def _unit_lower_solve(a, rhs, span):
    n = a.shape[0]
    r = lax.broadcasted_iota(jnp.int32, (n, n), 0)
    c = lax.broadcasted_iota(jnp.int32, (n, n), 1)
    in_diag = (r // SOLVE_BLOCK) == (c // SOLVE_BLOCK)
    a_off = jnp.where(in_diag, 0.0, a)
    p = jnp.where(in_diag, -a, 0.0)
    t = jnp.where(r == c, 1.0, 0.0) + p
    m = 2
    while m < min(SOLVE_BLOCK, span):
        p = _dot_precise(p, p)
        t = t + _dot_precise(p, t)
        m *= 2
    x = _dot_precise(t, rhs)
    for _ in range(-(-span // SOLVE_BLOCK) - 1):
        x = _dot_precise(t, rhs - _dot_precise(a_off, x))
    return x


def _dn_chunks(q, k, v, beta, dcol, drow, S, C):
    R = q.shape[0]
    r = lax.broadcasted_iota(jnp.int32, (R, R), 0)
    c = lax.broadcasted_iota(jnp.int32, (R, R), 1)
    low = jnp.where((r // C) == (c // C), r - c, -1)
    lmat = jnp.where(low >= 0, jnp.exp(jnp.where(low >= 0, dcol - drow, 0.0)), 0.0)
    kb = k * beta
    a = jnp.where(low > 0, _dot_nt(kb, k) * lmat, 0.0)
    x = _unit_lower_solve(a, jnp.concatenate([v * beta, kb * jnp.exp(dcol)], axis=1), C)
    u, w = x[:, :HEAD_DIM], x[:, HEAD_DIM:]
    qk = _dot_nt(q, k) * lmat
    qe = q * jnp.exp(dcol)
    outs, news = [], []
    for j in range(R // C):
        lo, hi = j * C, (j + 1) * C
        v_new = u[lo:hi] - _dot(w[lo:hi], S)
        news.append(v_new)
        pad = [jnp.zeros((R - hi, HEAD_DIM), F32)] if hi < R else []
        outs.append(_dot(qe[lo:hi], S) + _dot(qk[lo:hi], jnp.concatenate(news + pad, axis=0)))
        d_last = dcol[hi - 1:hi, :]
        kd = k[lo:hi] * jnp.exp(d_last - dcol[lo:hi])
        S = S * jnp.exp(d_last) + _dot(kd.T, v_new)
    return jnp.concatenate(outs, axis=0), S


def _dn_kernel(q_ref, k_ref, v_ref, gate_ref, bd_ref, bdt_ref, s0_ref, ng_ref, o_ref, s_ref, *, hb, chunk):
    hg = pl.program_id(1)
    T = q_ref.shape[0]
    R = bdt_ref.shape[2]

    def body(si, states):
        r0 = pl.multiple_of(si * R, R)
        bd = bd_ref[pl.ds(r0, R), :]
        out = []
        for hh in range(hb):
            sl = slice(hh * HEAD_DIM, (hh + 1) * HEAD_DIM)
            head = hg * hb + hh
            beta = _pick_col(bd, head)
            dcol = _pick_col(bd, N_HEADS + head)
            drow = bdt_ref[hh, pl.ds(si, 1), :]
            o, S = _dn_chunks(q_ref[pl.ds(r0, R), sl], k_ref[pl.ds(r0, R), sl], v_ref[pl.ds(r0, R), sl],
                              beta, dcol, drow, states[hh], chunk)
            y = _rms(o, ng_ref[...]) * _silu(gate_ref[pl.ds(r0, R), sl])
            o_ref[pl.ds(r0, R), sl] = y.astype(o_ref.dtype)
            out.append(S)
        return tuple(out)

    states = lax.fori_loop(0, T // R, body, tuple(s0_ref[hh] for hh in range(hb)))
    for hh in range(hb):
        s_ref[hh] = states[hh]


def _dn_attention(qkv, proj, bd, S0, norm_g, *, hb=2):
    B, T, _ = qkv.shape
    C = min(DN_CHUNK, T)
    R = min(DN_SUPER * C, T)
    assert T % R == 0 and R % C == 0
    W = hb * HEAD_DIM
    ng = N_HEADS // hb
    bdt = jnp.swapaxes(bd[:, :, N_HEADS:2 * N_HEADS], 1, 2).reshape(B, N_HEADS, T // R, R)
    seq = lambda off: pl.BlockSpec((None, T, W), lambda b, g: (b, 0, off + g))
    st = pl.BlockSpec((None, hb, HEAD_DIM, HEAD_DIM), lambda b, g: (b, g, 0, 0))
    return pl.pallas_call(
        functools.partial(_dn_kernel, hb=hb, chunk=C), grid=(B, ng),
        in_specs=[seq(0), seq(ng), seq(2 * ng),
                  pl.BlockSpec((None, T, W), lambda b, g: (b, 0, COL_DG * LANE // W + g)),
                  pl.BlockSpec((None, T, LANE), lambda b, g: (b, 0, 0)),
                  pl.BlockSpec((None, hb, T // R, R), lambda b, g: (b, g, 0, 0)),
                  st, pl.BlockSpec((1, HEAD_DIM), lambda b, g: (0, 0))],
        out_specs=(pl.BlockSpec((None, T, W), lambda b, g: (b, 0, g)), st),
        out_shape=(jax.ShapeDtypeStruct((B, T, GROUP_WIDTH), BF16),
                   jax.ShapeDtypeStruct((B, N_HEADS, HEAD_DIM, HEAD_DIM), F32)),
        compiler_params=_cparams("parallel", "parallel"),
    )(qkv, qkv, qkv, proj, bd, bdt, S0, norm_g.reshape(1, HEAD_DIM))


def _dn_step_prep_kernel(x_ref, prev_ref, w_ref, b_ref, pe_ref, alog_ref, dtb_ref, o_ref, bg_ref):
    K = w_ref.shape[0]
    y = b_ref[...] + x_ref[...] * w_ref[K - 1:K, :]
    for k in range(K - 1):
        y = y + prev_ref[:, k, :] * w_ref[k:k + 1, :]
    y = _silu(y)
    for part in range(3):
        for hh in range(N_HEADS):
            lo = part * GROUP_WIDTH + hh * HEAD_DIM
            seg = y[:, lo:lo + HEAD_DIM]
            if part == 0:
                seg = _l2n(seg) * HEAD_DIM ** -0.5
            elif part == 1:
                seg = _l2n(seg)
            o_ref[:, lo:lo + HEAD_DIM] = seg
    _, bg_ref[...] = _dn_gate_values(pe_ref[...], alog_ref, dtb_ref)


def _dn_step_prep(x, prev, conv_w, conv_b, pe, a_log, dt_bias):
    B, W = x.shape
    alog, dtb = _gate_rows(a_log, dt_bias)
    return pl.pallas_call(
        _dn_step_prep_kernel,
        out_shape=(jax.ShapeDtypeStruct((B, W), F32), jax.ShapeDtypeStruct((B, LANE), F32)),
    )(x, prev, conv_w, conv_b.reshape(1, W), pe, alog, dtb)


def _dn_step_kernel(q_ref, k_ref, v_ref, gate_ref, bg_ref, s0_ref, ng_ref, o_ref, s_ref):
    h = pl.program_id(1)
    q, k, v = q_ref[...], k_ref[...], v_ref[...]
    bg = bg_ref[...]
    beta = _pick_col(bg, h)
    e = jnp.exp(_pick_col(bg, N_HEADS + h))
    S = s0_ref[...]
    rows = jnp.concatenate([k * beta * e, q * e, jnp.zeros((SUBLANE - 2, HEAD_DIM), F32)], axis=0)
    rs = _dot(rows, S)
    v_new = v * beta - rs[0:1, :]
    o = rs[1:2, :] + jnp.sum(q * k, axis=-1, keepdims=True) * v_new
    r = lax.broadcasted_iota(jnp.int32, (HEAD_DIM, HEAD_DIM), 0)
    c = lax.broadcasted_iota(jnp.int32, (HEAD_DIM, HEAD_DIM), 1)
    kcol = jnp.sum(jnp.where(r == c, jnp.broadcast_to(k, (HEAD_DIM, HEAD_DIM)), 0.0), axis=-1, keepdims=True)
    s_ref[...] = S * e + kcol * v_new
    o_ref[...] = _rms(o, ng_ref[...]) * _silu(gate_ref[...])


def _dn_step(qkv, gate, bg, S0, norm_g):
    B = qkv.shape[0]
    head = lambda off: pl.BlockSpec((None, 1, HEAD_DIM), lambda b, h: (b, 0, off + h))
    st = pl.BlockSpec((None, None, HEAD_DIM, HEAD_DIM), lambda b, h: (b, h, 0, 0))
    return pl.pallas_call(
        _dn_step_kernel, grid=(B, N_HEADS),
        in_specs=[head(0), head(N_HEADS), head(2 * N_HEADS), head(0),
                  pl.BlockSpec((None, 1, LANE), lambda b, h: (b, 0, 0)), st,
                  pl.BlockSpec((1, HEAD_DIM), lambda b, h: (0, 0))],
        out_specs=(head(0), st),
        out_shape=(jax.ShapeDtypeStruct((B, 1, GROUP_WIDTH), F32),
                   jax.ShapeDtypeStruct((B, N_HEADS, HEAD_DIM, HEAD_DIM), F32)),
        compiler_params=_cparams("parallel", "parallel"),
    )(qkv, qkv, qkv, gate, bg, S0, norm_g.reshape(1, HEAD_DIM))


def _project_in(h2, wb, layer):
    proj = _matmul(h2, wb['w_in'], layer, tm=1024, tn=1024, n_out=IN_MAIN)
    pe = _matmul(h2, wb['w_in_tail'], layer, tm=1024, tn=LANE)
    return proj, pe


def _state_to_rows(s):
    B = s.shape[0]
    return jnp.moveaxis(s, -1, 1).reshape(B, 2, -1)


def _rows_to_state(s):
    B = s.shape[0]
    return jnp.moveaxis(s.reshape(B, 2, S5_GROUPS, S5_STATE), 1, -1)


def _kv_out(proj, col, B, T):
    w = 2 * GROUP_WIDTH
    lo = col * LANE + GROUP_WIDTH
    return proj[:, lo:lo + w].reshape(B, T, 2, N_HEADS, HEAD_DIM)


def _layer(carry, lw, consts):
    xp, xs = carry
    (mod, st_s5, st_dn, st_dn_conv, st_ffn, layer) = lw['dyn']
    W = lw['w']
    cache_sb, cache_mb, page_table, slopes, wb = consts
    Bp, Tp, D = xp.shape
    Bs = xs.shape[0]
    w_glu = W['s5_w_glu'].astype(BF16)
    F2 = wb['ffn_w_up'].shape[2]

    s5p = _s5_params(W['s5_lambda_re'], W['s5_lambda_im'], W['s5_log_dt'], W['s5_b_re'], W['s5_b_im'])
    packed = _s5_pack(*s5p, W['s5_c_re'], W['s5_c_im'])

    def mods(m):
        m = m.reshape(m.shape[0], N_MOD, 1, D)
        return [m[:, i] for i in range(N_MOD)]

    mp, ms = mods(mod[:Bp]), mods(mod[Bp:Bp + Bs])
    row = lambda a: a.reshape(1, -1)

    h = _prenorm(xp, row(W['g_pre_mix']), mp[1], mp[0])
    proj, pe = _project_in(h.reshape(Bp * Tp, D), wb, layer)
    proj3, pe3 = proj.reshape(Bp, Tp, IN_MAIN), pe.reshape(Bp, Tp, LANE)
    y_s5, hs = _s5_scan(proj3, jnp.zeros((Bp, 2, S5_GROUPS * S5_STATE), F32), packed, W['s5_d'])
    y_s5 = _glu(y_s5.reshape(Bp * Tp, GROUP_WIDTH), w_glu, W['s5_b_glu'])
    y_sb = _sb_attention(proj3)
    y_mb = _moba_attention(proj3, slopes)
    K_dn = W['dn_conv_w'].shape[0]
    qkv_dn = _dn_prep(proj3, jnp.zeros((Bp, K_dn - 1, 3 * GROUP_WIDTH), F32), W['dn_conv_w'], W['dn_conv_b'])
    bd = _dn_gates(pe3, W['dn_a_log'], W['dn_dt_bias'])
    y_dn, dn_S = _dn_attention(qkv_dn, proj3, bd, jnp.zeros((Bp, N_HEADS, HEAD_DIM, HEAD_DIM), F32), W['dn_norm_g'])
    ycat = jnp.concatenate([y_s5.reshape(Bp, Tp, GROUP_WIDTH), y_sb, y_mb, y_dn], axis=-1)
    y = _matmul(ycat.reshape(Bp * Tp, 4 * GROUP_WIDTH), wb['w_out'], layer, tm=1024, tn=1024)
    xp, h = _resid_norm(xp, y.reshape(Bp, Tp, D), mp[2], row(W['g_post_mix']), row(W['g_pre_ffn']), mp[4], mp[3])
    up = _matmul(h.reshape(Bp * Tp, D), wb['ffn_w_up'], layer, tm=1024, tn=512).reshape(Bp, Tp, F2)
    K_f = W['ffn_conv_w'].shape[0]
    act = _ffn_gate(up, jnp.zeros((Bp, K_f - 1, F2), F32), W['ffn_conv_w'], W['ffn_conv_b'])
    y = _matmul(act.reshape(Bp * Tp, F2 // 2), wb['ffn_w_down'], layer, tm=1024, tn=512, tk=F2 // 4)
    xp = _resid(xp, y.reshape(Bp, Tp, D), mp[5], row(W['g_post_ffn']))
    dn_lo = COL_DN * LANE
    out_p = (_kv_out(proj, COL_SB, Bp, Tp), _kv_out(proj, COL_MB, Bp, Tp), _rows_to_state(hs), dn_S,
             proj3[:, Tp - (K_dn - 1):, dn_lo:dn_lo + 3 * GROUP_WIDTH], up[:, Tp - (K_f - 1):, :])

    h = _prenorm(xs, row(W['g_pre_mix']), ms[1], ms[0])
    proj, pe = _project_in(h.reshape(Bs, D), wb, layer)
    y_s5, hs = _s5_step(proj[:, :GROUP_WIDTH], _state_to_rows(st_s5), packed, W['s5_d'])
    y_s5 = _glu(y_s5, w_glu, W['s5_b_glu'])
    heads = lambda col: proj[:, col * LANE:col * LANE + GROUP_WIDTH].reshape(Bs, N_HEADS, HEAD_DIM)
    y_sb = _sb_step(heads(COL_SB), cache_sb, page_table, layer)
    y_mb = _moba_step(heads(COL_MB), heads(COL_MB + N_HEADS), heads(COL_MB + 2 * N_HEADS),
                      cache_mb, page_table, layer, slopes)
    x_dn = proj[:, dn_lo:dn_lo + 3 * GROUP_WIDTH]
    qkv_dn, bg = _dn_step_prep(x_dn, st_dn_conv, W['dn_conv_w'], W['dn_conv_b'], pe, W['dn_a_log'], W['dn_dt_bias'])
    gate_dn = proj[:, COL_DG * LANE:COL_DG * LANE + GROUP_WIDTH]
    y_dn, dn_S = _dn_step(qkv_dn.reshape(Bs, 1, -1), gate_dn.reshape(Bs, 1, -1), bg.reshape(Bs, 1, LANE),
                          st_dn, W['dn_norm_g'])
    ycat = jnp.concatenate([y_s5, y_sb.reshape(Bs, -1).astype(BF16), y_mb.reshape(Bs, -1).astype(BF16),
                            y_dn.reshape(Bs, -1).astype(BF16)], axis=-1)
    y = _matmul(ycat, wb['w_out'], layer, tm=Bs, tn=1024)
    xs, h = _resid_norm(xs, y.reshape(Bs, 1, D), ms[2], row(W['g_post_mix']), row(W['g_pre_ffn']), ms[4], ms[3])
    up = _matmul(h.reshape(Bs, D), wb['ffn_w_up'], layer, tm=Bs, tn=512)
    act = _ffn_gate_step(up, st_ffn, W['ffn_conv_w'], W['ffn_conv_b'])
    y = _matmul(act, wb['ffn_w_down'], layer, tm=Bs, tn=512, tk=F2 // 4)
    xs = _resid(xs, y.reshape(Bs, 1, D), ms[5], row(W['g_post_ffn']))
    out_s = (_kv_out(proj, COL_SB, Bs, 1), _kv_out(proj, COL_MB, Bs, 1), _rows_to_state(hs), dn_S,
             jnp.concatenate([st_dn_conv[:, 1:], x_dn[:, None, :]], axis=1),
             jnp.concatenate([st_ffn[:, 1:], up[:, None, :]], axis=1))
    return (xp, xs), (out_p, out_s)


def kernel(x_prompt, x_sample, cache_sb_kv, cache_moba_kv, state_s5, state_dn, state_dn_conv, state_ffn_conv, page_table, c_prompt, c_sample, w_in, w_out, w_ada, b_ada, g_pre_mix, g_post_mix, g_pre_ffn, g_post_ffn, s5_lambda_re, s5_lambda_im, s5_log_dt, s5_b_re, s5_b_im, s5_c_re, s5_c_im, s5_d, s5_w_glu, s5_b_glu, dn_conv_w, dn_conv_b, dn_a_log, dn_dt_bias, dn_norm_g, ffn_w_up, ffn_conv_w, ffn_conv_b, ffn_w_down):
    depth = w_in.shape[0]
    Bp, Bs = x_prompt.shape[0], x_sample.shape[0]
    rows = -(-(Bp + Bs) // SUBLANE) * SUBLANE
    c_all = jnp.concatenate([c_prompt, c_sample, jnp.zeros((rows - Bp - Bs, c_prompt.shape[1]), F32)], axis=0)
    mod = _ada_all(c_all, w_ada, b_ada)
    slopes = 2.0 ** (-8.0 * (jnp.arange(N_HEADS, dtype=F32) + 1.0) / N_HEADS)
    tail = w_in[:, :, IN_MAIN:]
    wb = dict(w_in=w_in.astype(BF16), w_out=w_out.astype(BF16), ffn_w_up=ffn_w_up.astype(BF16),
              ffn_w_down=ffn_w_down.astype(BF16),
              w_in_tail=jnp.pad(tail, ((0, 0), (0, 0), (0, LANE - tail.shape[2]))).astype(BF16))
    weights = dict(g_pre_mix=g_pre_mix, g_post_mix=g_post_mix, g_pre_ffn=g_pre_ffn,
                   g_post_ffn=g_post_ffn, s5_lambda_re=s5_lambda_re, s5_lambda_im=s5_lambda_im,
                   s5_log_dt=s5_log_dt, s5_b_re=s5_b_re, s5_b_im=s5_b_im, s5_c_re=s5_c_re, s5_c_im=s5_c_im,
                   s5_d=s5_d, s5_w_glu=s5_w_glu, s5_b_glu=s5_b_glu, dn_conv_w=dn_conv_w, dn_conv_b=dn_conv_b,
                   dn_a_log=dn_a_log, dn_dt_bias=dn_dt_bias, dn_norm_g=dn_norm_g,
                   ffn_conv_w=ffn_conv_w, ffn_conv_b=ffn_conv_b)
    dyn = (mod, state_s5, state_dn, state_dn_conv, state_ffn_conv, jnp.arange(depth, dtype=jnp.int32))
    consts = (cache_sb_kv, cache_moba_kv, page_table, slopes, wb)
    step = lambda carry, lw: _layer(carry, lw, consts)
    (xp, xs), (out_p, out_s) = lax.scan(step, (x_prompt, x_sample), dict(dyn=dyn, w=weights))
    outs = [xp, xs]
    for p, s in zip(out_p, out_s):
        outs += [p, s]
    return tuple(outs)
```

```python
import functools

import jax
import jax.numpy as jnp
from jax import lax
from jax.experimental import pallas as pl
from jax.experimental.pallas import tpu as pltpu

F32 = jnp.float32
BF16 = jnp.bfloat16

HEAD_DIM = 128
N_HEADS = 8
GROUP_WIDTH = N_HEADS * HEAD_DIM
S5_CH = 16
S5_STATE = 64
S5_GROUPS = GROUP_WIDTH // S5_CH
S5_GT = 8
S5_NT = S5_GROUPS // S5_GT
S5_TS = S5_GT * S5_STATE
MOBA_BLOCK = 256
MOBA_TOPK = 3
DN_CHUNK = 64
N_MOD = 6
EPS = 1e-6
ATT_SCALE = HEAD_DIM ** -0.5
NEG = -1e30

LANE = 128
SUBLANE = 8
VMEM_LIMIT_BYTES = 56 * 2 ** 20

COL_S5 = 0
COL_SB = GROUP_WIDTH // LANE
COL_MB = 4 * GROUP_WIDTH // LANE
COL_DN = 7 * GROUP_WIDTH // LANE
COL_DG = 10 * GROUP_WIDTH // LANE
IN_MAIN = 11 * GROUP_WIDTH


def _cparams(*sem):
    return pltpu.CompilerParams(dimension_semantics=sem, vmem_limit_bytes=VMEM_LIMIT_BYTES)


def _dot(a, b):
    return jnp.dot(a.astype(BF16), b.astype(BF16), preferred_element_type=F32)


def _dot_nt(a, b):
    return lax.dot_general(a.astype(BF16), b.astype(BF16), (((1,), (1,)), ((), ())),
                           preferred_element_type=F32)


def _split(x):
    hi = x.astype(BF16)
    lo = (x - hi.astype(F32)).astype(BF16)
    return hi, lo


def _dot_x01(x, m01):
    hi, lo = _split(x)
    return (jnp.dot(hi, m01, preferred_element_type=F32)
            + jnp.dot(lo, m01, preferred_element_type=F32))


def _softplus(z):
    return jnp.maximum(z, 0.0) + jnp.log1p(jnp.exp(-jnp.abs(z)))


def _silu(x):
    return x * jax.nn.sigmoid(x)


def _shift_rows(x, d):
    rows = lax.broadcasted_iota(jnp.int32, x.shape, 0)
    return jnp.where(rows >= d, pltpu.roll(x, d, 0), 0.0)


def _pick_dot(precise):
    return _dot_precise if precise else _dot


def _mm_kernel(l_ref, x_ref, w_ref, o_ref, *, precise):
    o_ref[...] = _pick_dot(precise)(x_ref[...], w_ref[...]).astype(o_ref.dtype)


def _mm_acc_kernel(l_ref, x_ref, w_ref, o_ref, acc_ref, *, precise):
    k = pl.program_id(2)
    part = _pick_dot(precise)(x_ref[...], w_ref[...])

    @pl.when(k == 0)
    def _():
        acc_ref[...] = part

    @pl.when(k > 0)
    def _():
        acc_ref[...] += part

    @pl.when(k == pl.num_programs(2) - 1)
    def _():
        o_ref[...] = acc_ref[...].astype(o_ref.dtype)


def _matmul(x, w, layer, *, tm, tn, tk=None, n_out=None, out_dtype=F32, precise=False):
    M, K = x.shape
    n_out = w.shape[2] if n_out is None else n_out
    tm, tn = min(tm, M), min(tn, n_out)
    tk = K if tk is None else tk
    assert M % tm == 0 and n_out % tn == 0 and K % tk == 0
    nk = K // tk
    lidx = jnp.reshape(layer, (1,)).astype(jnp.int32)
    if nk == 1:
        return pl.pallas_call(
            functools.partial(_mm_kernel, precise=precise),
            grid_spec=pltpu.PrefetchScalarGridSpec(
                num_scalar_prefetch=1, grid=(M // tm, n_out // tn),
                in_specs=[pl.BlockSpec((tm, K), lambda i, j, l: (i, 0)),
                          pl.BlockSpec((None, K, tn), lambda i, j, l: (l[0], 0, j))],
                out_specs=pl.BlockSpec((tm, tn), lambda i, j, l: (i, j))),
            out_shape=jax.ShapeDtypeStruct((M, n_out), out_dtype),
            compiler_params=_cparams("parallel", "parallel"),
        )(lidx, x, w)
    return pl.pallas_call(
        functools.partial(_mm_acc_kernel, precise=precise),
        grid_spec=pltpu.PrefetchScalarGridSpec(
            num_scalar_prefetch=1, grid=(M // tm, n_out // tn, nk),
            in_specs=[pl.BlockSpec((tm, tk), lambda i, j, k, l: (i, k)),
                      pl.BlockSpec((None, tk, tn), lambda i, j, k, l: (l[0], k, j))],
            out_specs=pl.BlockSpec((tm, tn), lambda i, j, k, l: (i, j)),
            scratch_shapes=[pltpu.VMEM((tm, tn), F32)]),
        out_shape=jax.ShapeDtypeStruct((M, n_out), out_dtype),
        compiler_params=_cparams("parallel", "parallel", "arbitrary"),
    )(lidx, x, w)


def _ada_kernel(c_ref, w_ref, b_ref, o_ref):
    o_ref[...] = _dot(_silu(c_ref[...]), w_ref[...]) + b_ref[...]


def _ada_all(c, w_ada, b_ada, *, tn=512):
    R, D = c.shape
    L, _, N = w_ada.shape
    tn = min(tn, N)
    return pl.pallas_call(
        _ada_kernel,
        grid=(L, N // tn),
        in_specs=[pl.BlockSpec((R, D), lambda l, j: (0, 0)),
                  pl.BlockSpec((None, D, tn), lambda l, j: (l, 0, j)),
                  pl.BlockSpec((None, 1, tn), lambda l, j: (l, 0, j))],
        out_specs=pl.BlockSpec((None, R, tn), lambda l, j: (l, 0, j)),
        out_shape=jax.ShapeDtypeStruct((L, R, N), F32),
        compiler_params=_cparams("parallel", "parallel"),
    )(c, w_ada, b_ada.reshape(L, 1, N))


def _rms(x, g):
    return x * lax.rsqrt(jnp.mean(x * x, axis=-1, keepdims=True) + EPS) * g


def _prenorm_kernel(x_ref, g_ref, sc_ref, sh_ref, h_ref):
    h = _rms(x_ref[...], g_ref[...]) * (1.0 + sc_ref[...]) + sh_ref[...]
    h_ref[...] = h.astype(h_ref.dtype)


def _resid_norm_kernel(x_ref, y_ref, gate_ref, gpost_ref, gnext_ref, sc_ref, sh_ref, xo_ref, h_ref):
    x = x_ref[...] + gate_ref[...] * _rms(y_ref[...], gpost_ref[...])
    xo_ref[...] = x
    h = _rms(x, gnext_ref[...]) * (1.0 + sc_ref[...]) + sh_ref[...]
    h_ref[...] = h.astype(h_ref.dtype)


def _resid_kernel(x_ref, y_ref, gate_ref, gpost_ref, xo_ref):
    xo_ref[...] = x_ref[...] + gate_ref[...] * _rms(y_ref[...], gpost_ref[...])


def _row_specs(T, D, tr):
    act = pl.BlockSpec((None, tr, D), lambda b, i: (b, i, 0))
    per_seq = pl.BlockSpec((None, 1, D), lambda b, i: (b, 0, 0))
    gain = pl.BlockSpec((1, D), lambda b, i: (0, 0))
    return act, per_seq, gain


def _prenorm(x, g, scale, shift, *, tr=256, out_dtype=BF16):
    B, T, D = x.shape
    tr = min(tr, T)
    act, per_seq, gain = _row_specs(T, D, tr)
    return pl.pallas_call(
        _prenorm_kernel, grid=(B, T // tr),
        in_specs=[act, gain, per_seq, per_seq], out_specs=act,
        out_shape=jax.ShapeDtypeStruct((B, T, D), out_dtype),
        compiler_params=_cparams("parallel", "parallel"),
    )(x, g, scale, shift)


def _resid_norm(x, y, gate, g_post, g_next, scale, shift, *, tr=256, out_dtype=BF16):
    B, T, D = x.shape
    tr = min(tr, T)
    act, per_seq, gain = _row_specs(T, D, tr)
    return pl.pallas_call(
        _resid_norm_kernel, grid=(B, T // tr),
        in_specs=[act, act, per_seq, gain, gain, per_seq, per_seq], out_specs=(act, act),
        out_shape=(jax.ShapeDtypeStruct((B, T, D), F32), jax.ShapeDtypeStruct((B, T, D), out_dtype)),
        compiler_params=_cparams("parallel", "parallel"),
    )(x, y, gate, g_post, g_next, scale, shift)


def _resid(x, y, gate, g_post, *, tr=256):
    B, T, D = x.shape
    tr = min(tr, T)
    act, per_seq, gain = _row_specs(T, D, tr)
    return pl.pallas_call(
        _resid_kernel, grid=(B, T // tr),
        in_specs=[act, act, per_seq, gain], out_specs=act,
        out_shape=jax.ShapeDtypeStruct((B, T, D), F32),
        compiler_params=_cparams("parallel", "parallel"),
    )(x, y, gate, g_post)


def _conv_rows(x, prev_rows, w_ref, b_ref):
    K = w_ref.shape[0]
    rows = lax.broadcasted_iota(jnp.int32, x.shape, 0)
    y = b_ref[...] + x * w_ref[K - 1:K, :]
    for d in range(1, K):
        xs = pltpu.roll(x, d, 0)
        for r in range(d):
            xs = jnp.where(rows == r, prev_rows[d - r - 1], xs)
        y = y + xs * w_ref[K - 1 - d:K - d, :]
    return y


def _prev_rows(first, state_ref, tail_ref, n):
    ns, nt = state_ref.shape[0], tail_ref.shape[0]
    return [jnp.where(first, state_ref[ns - 1 - k:ns - k, :], tail_ref[nt - 1 - k:nt - k, :])
            for k in range(n)]


def _ffn_gate_kernel(a_ref, g_ref, ta_ref, tg_ref, sa_ref, sg_ref, wa_ref, wg_ref, ba_ref, bg_ref, o_ref):
    first = pl.program_id(1) == 0
    K = wa_ref.shape[0]
    a = _conv_rows(a_ref[...], _prev_rows(first, sa_ref, ta_ref, K - 1), wa_ref, ba_ref)
    g = _conv_rows(g_ref[...], _prev_rows(first, sg_ref, tg_ref, K - 1), wg_ref, bg_ref)
    o_ref[...] = (_silu(g) * a).astype(o_ref.dtype)


def _ffn_gate(up, prev, conv_w, conv_b, *, tt=128, tc=5504):
    B, T, F2 = up.shape
    F = F2 // 2
    K = conv_w.shape[0]
    tt, tc = min(tt, T), min(tc, F)
    assert T % tt == 0 and F % tc == 0 and tt % SUBLANE == 0
    nc = F // tc
    tb = tt // SUBLANE
    cur = lambda off: pl.BlockSpec((None, tt, tc), lambda b, i, j: (b, i, j + off))
    tail = lambda off: pl.BlockSpec((None, SUBLANE, tc), lambda b, i, j: (b, jnp.maximum(i * tb - 1, 0), j + off))
    st = lambda off: pl.BlockSpec((None, K - 1, tc), lambda b, i, j: (b, 0, j + off))
    wsp = lambda off: pl.BlockSpec((K, tc), lambda b, i, j: (0, j + off))
    bsp = lambda off: pl.BlockSpec((1, tc), lambda b, i, j: (0, j + off))
    return pl.pallas_call(
        _ffn_gate_kernel, grid=(B, T // tt, nc),
        in_specs=[cur(0), cur(nc), tail(0), tail(nc), st(0), st(nc), wsp(0), wsp(nc), bsp(0), bsp(nc)],
        out_specs=pl.BlockSpec((None, tt, tc), lambda b, i, j: (b, i, j)),
        out_shape=jax.ShapeDtypeStruct((B, T, F), BF16),
        compiler_params=_cparams("parallel", "parallel", "parallel"),
    )(up, up, up, up, prev, prev, conv_w, conv_w, conv_b.reshape(1, F2), conv_b.reshape(1, F2))


def _ffn_gate_step_kernel(up_ref, prev_ref, w_ref, b_ref, o_ref):
    F = o_ref.shape[1]
    K = w_ref.shape[0]
    y = b_ref[...] + up_ref[...] * w_ref[K - 1:K, :]
    for k in range(K - 1):
        y = y + prev_ref[:, k, :] * w_ref[k:k + 1, :]
    o_ref[...] = (_silu(y[:, F:]) * y[:, :F]).astype(o_ref.dtype)


def _ffn_gate_step(up, prev, conv_w, conv_b):
    B, F2 = up.shape
    return pl.pallas_call(
        _ffn_gate_step_kernel,
        out_shape=jax.ShapeDtypeStruct((B, F2 // 2), F32),
        compiler_params=pltpu.CompilerParams(vmem_limit_bytes=VMEM_LIMIT_BYTES),
    )(up, prev, conv_w, conv_b.reshape(1, F2))


def _s5_param_kernel(lr_ref, li_ref, dt_ref, br_ref, bi_ref, ar_ref, ai_ref, bbr_ref, bbi_ref):
    lr, li = lr_ref[...], li_ref[...]
    dt = jnp.exp(dt_ref[...])
    mag = jnp.exp(lr * dt)
    ar, ai = mag * jnp.cos(li * dt), mag * jnp.sin(li * dt)
    den = lr * lr + li * li
    fr = ((ar - 1.0) * lr + ai * li) / den
    fi = (ai * lr - (ar - 1.0) * li) / den
    ar_ref[...], ai_ref[...] = ar, ai
    bbr_ref[...] = fr * br_ref[...] - fi * bi_ref[...]
    bbi_ref[...] = fr * bi_ref[...] + fi * br_ref[...]


def _s5_params(lam_re, lam_im, log_dt, b_re, b_im):
    G, P = lam_re.shape
    CH = b_re.shape[-1]
    col = lambda a: a.reshape(G * P, 1)
    dt = jnp.broadcast_to(log_dt[:, None], (G, P))
    return pl.pallas_call(
        _s5_param_kernel,
        out_shape=(jax.ShapeDtypeStruct((G * P, 1), F32),) * 2 + (jax.ShapeDtypeStruct((G * P, CH), F32),) * 2,
    )(col(lam_re), col(lam_im), col(dt), b_re.reshape(G * P, CH), b_im.reshape(G * P, CH))


def _s5_pack(ar, ai, bbr, bbi, c_re, c_im):
    eye = jnp.eye(S5_GT, dtype=F32)
    a_re = ar.reshape(S5_NT, 1, S5_TS)
    a_im = ai.reshape(S5_NT, 1, S5_TS)

    def b_dense(bb):
        bb = bb.reshape(S5_NT, S5_GT, S5_STATE, S5_CH)
        return jnp.einsum('tgpc,gh->tgchp', bb, eye).reshape(S5_NT, S5_GT * S5_CH, S5_TS)

    def c_dense(c):
        c = c.reshape(S5_NT, S5_GT, S5_CH, S5_STATE)
        return jnp.einsum('tgcp,gh->tgphc', c, eye).reshape(S5_NT, S5_TS, S5_GT * S5_CH)

    bd = jnp.concatenate([b_dense(bbr), b_dense(bbi)], axis=-1)
    return a_re, a_im, bd, c_dense(c_re), c_dense(c_im)


def _cmul(ar, ai, br, bi):
    return ar * br - ai * bi, ar * bi + ai * br


def _s5_kernel(u_ref, h0_ref, are_ref, aim_ref, bd_ref, cre_ref, cim_ref, d_ref, y_ref, hs_ref,
               pr_ref, pi_ref, carry_ref, h_ref):
    c = pl.program_id(2)
    L = u_ref.shape[0]
    G = pr_ref.shape[0]

    @pl.when(c == 0)
    def _():
        pr_ref[0:1, :] = are_ref[...]
        pi_ref[0:1, :] = aim_ref[...]
        n = 1
        while n < G:
            tr, ti = pr_ref[n - 1:n, :], pi_ref[n - 1:n, :]
            qr, qi = _cmul(pr_ref[0:n, :], pi_ref[0:n, :], tr, ti)
            pr_ref[n:2 * n, :] = qr
            pi_ref[n:2 * n, :] = qi
            n *= 2
        carry_ref[...] = h0_ref[...]

    u = u_ref[...]
    h_ref[...] = _dot(u, bd_ref[...])

    def group(t, carry):
        r0 = pl.multiple_of(t * G, G)
        hr, hi = h_ref[pl.ds(r0, G), :S5_TS], h_ref[pl.ds(r0, G), S5_TS:]
        d = 1
        while d < G:
            sr, si = _cmul(pr_ref[d - 1:d, :], pi_ref[d - 1:d, :], _shift_rows(hr, d), _shift_rows(hi, d))
            hr, hi = hr + sr, hi + si
            d *= 2
        cr, ci = _cmul(pr_ref[...], pi_ref[...], carry[0], carry[1])
        hr, hi = hr + cr, hi + ci
        h_ref[pl.ds(r0, G), :S5_TS] = hr
        h_ref[pl.ds(r0, G), S5_TS:] = hi
        return hr[G - 1:G, :], hi[G - 1:G, :]

    cr, ci = lax.fori_loop(0, L // G, group, (carry_ref[0:1, :], carry_ref[1:2, :]), unroll=4)
    carry_ref[0:1, :] = cr
    carry_ref[1:2, :] = ci
    y = _dot(h_ref[:, :S5_TS], cre_ref[...]) - _dot(h_ref[:, S5_TS:], cim_ref[...]) + u * d_ref[...]
    y_ref[...] = jax.nn.gelu(y)

    @pl.when(c == pl.num_programs(2) - 1)
    def _():
        hs_ref[...] = carry_ref[...]


def _s5_scan(proj, h0, packed, d_skip, *, L=256):
    B, T, _ = proj.shape
    a_re, a_im, bd, cre, cim = packed
    L = min(L, T)
    G = min(SUBLANE, L)
    assert T % L == 0 and L % G == 0 and G & (G - 1) == 0
    tile = lambda t: pl.BlockSpec((None,) + t, lambda b, g, c: (g, 0, 0))
    return pl.pallas_call(
        _s5_kernel, grid=(B, S5_NT, T // L),
        in_specs=[pl.BlockSpec((None, L, LANE), lambda b, g, c: (b, c, COL_S5 + g)),
                  pl.BlockSpec((None, 2, S5_TS), lambda b, g, c: (b, 0, g)),
                  tile((1, S5_TS)), tile((1, S5_TS)), tile((LANE, 2 * S5_TS)),
                  tile((S5_TS, LANE)), tile((S5_TS, LANE)),
                  pl.BlockSpec((1, LANE), lambda b, g, c: (0, g))],
        out_specs=(pl.BlockSpec((None, L, LANE), lambda b, g, c: (b, c, g)),
                   pl.BlockSpec((None, 2, S5_TS), lambda b, g, c: (b, 0, g))),
        out_shape=(jax.ShapeDtypeStruct((B, T, GROUP_WIDTH), F32),
                   jax.ShapeDtypeStruct((B, 2, S5_GROUPS * S5_STATE), F32)),
        scratch_shapes=[pltpu.VMEM((G, S5_TS), F32), pltpu.VMEM((G, S5_TS), F32),
                        pltpu.VMEM((2, S5_TS), F32), pltpu.VMEM((L, 2 * S5_TS), F32)],
        compiler_params=_cparams("parallel", "parallel", "arbitrary"),
    )(proj, h0, a_re, a_im, bd, cre, cim, d_skip.reshape(1, GROUP_WIDTH))


def _s5_step_kernel(u_ref, h0_ref, are_ref, aim_ref, bd_ref, cre_ref, cim_ref, d_ref, y_ref, hs_ref):
    for g in range(S5_NT):
        u = u_ref[:, g * LANE:(g + 1) * LANE]
        bu = _dot(u, bd_ref[g])
        sl = slice(g * S5_TS, (g + 1) * S5_TS)
        ar, ai = are_ref[g], aim_ref[g]
        pr, pi = _cmul(ar, ai, h0_ref[:, 0, sl], h0_ref[:, 1, sl])
        hr, hi = bu[:, :S5_TS] + pr, bu[:, S5_TS:] + pi
        hs_ref[:, 0, sl] = hr
        hs_ref[:, 1, sl] = hi
        y = _dot(hr, cre_ref[g]) - _dot(hi, cim_ref[g]) + u * d_ref[:, g * LANE:(g + 1) * LANE]
        y_ref[:, g * LANE:(g + 1) * LANE] = jax.nn.gelu(y)


def _s5_step(u, h0, packed, d_skip):
    B = u.shape[0]
    a_re, a_im, bd, cre, cim = packed
    return pl.pallas_call(
        _s5_step_kernel,
        out_shape=(jax.ShapeDtypeStruct((B, GROUP_WIDTH), F32),
                   jax.ShapeDtypeStruct((B, 2, S5_GROUPS * S5_STATE), F32)),
        compiler_params=pltpu.CompilerParams(vmem_limit_bytes=VMEM_LIMIT_BYTES),
    )(u, h0, a_re, a_im, bd, cre, cim, d_skip.reshape(1, GROUP_WIDTH))


def _glu_kernel(y_ref, w_ref, b_ref, o_ref, *, precise):
    y = y_ref[...]
    o_ref[...] = (y * jax.nn.sigmoid(_pick_dot(precise)(y, w_ref[...]) + b_ref[...])).astype(o_ref.dtype)


def _glu(y, w, b, *, tm=1024, precise=False):
    M, N = y.shape
    tm = min(tm, M)
    return pl.pallas_call(
        functools.partial(_glu_kernel, precise=precise), grid=(M // tm,),
        in_specs=[pl.BlockSpec((tm, N), lambda i: (i, 0)),
                  pl.BlockSpec((N, N), lambda i: (0, 0)),
                  pl.BlockSpec((1, N), lambda i: (0, 0))],
        out_specs=pl.BlockSpec((tm, N), lambda i: (i, 0)),
        out_shape=jax.ShapeDtypeStruct((M, N), F32 if precise else BF16),
        compiler_params=_cparams("parallel"),
    )(y, w, b.reshape(1, N))


def _sb_block(q, kb, vb, upper, run, acc, mask):
    z = [_dot_nt(qi, ki) * ATT_SCALE for qi, ki in zip(q, kb)]
    sp = [_softplus(zi) for zi in z]
    log_keep = [-si if mask is None else jnp.where(mask, -si, 0.0) for si in sp]
    later = [_dot_x01(li, upper) for li in log_keep]
    w = [jnp.exp(zi - si + li + ri) for zi, si, li, ri in zip(z, sp, later, run)]
    if mask is not None:
        w = [jnp.where(mask, wi, 0.0) for wi in w]
    acc = [ai + _dot(wi, vi) for ai, wi, vi in zip(acc, w, vb)]
    run = [ri + li[:, 0:1] + ki[:, 0:1] for ri, li, ki in zip(run, later, log_keep)]
    return run, acc


def _sb_kernel(q_ref, k_ref, v_ref, o_ref, *, tk, hb):
    i = pl.program_id(2)
    tq = q_ref.shape[0]
    nd = tq // tk
    lanes = [slice(h * HEAD_DIM, (h + 1) * HEAD_DIM) for h in range(hb)]
    q = [q_ref[:, sl] for sl in lanes]
    ur = lax.broadcasted_iota(jnp.int32, (tk, tk), 0)
    uc = lax.broadcasted_iota(jnp.int32, (tk, tk), 1)
    upper = jnp.where(ur > uc, 1.0, 0.0).astype(BF16)
    r = lax.broadcasted_iota(jnp.int32, (tq, tk), 0)
    c = lax.broadcasted_iota(jnp.int32, (tq, tk), 1)

    def block(start, run, acc, mask):
        rows = pl.ds(start, tk)
        return _sb_block(q, [k_ref[rows, sl] for sl in lanes], [v_ref[rows, sl] for sl in lanes], upper,
                         run, acc, mask)

    run = [jnp.zeros((tq, 1), F32) for _ in lanes]
    acc = [jnp.zeros((tq, HEAD_DIM), F32) for _ in lanes]
    for d in reversed(range(nd)):
        run, acc = block(pl.multiple_of(i * tq + d * tk, tk), run, acc, c + d * tk < r)

    def body(jj, carry):
        run, acc = block(pl.multiple_of((i * nd - 1 - jj) * tk, tk), list(carry[0]), list(carry[1]), None)
        return tuple(run), tuple(acc)

    run, acc = lax.fori_loop(0, i * nd, body, (tuple(run), tuple(acc)))
    for ai, sl in zip(acc, lanes):
        o_ref[:, sl] = ai.astype(o_ref.dtype)


def _sb_attention(proj, *, tq=512, tk=256, hb=2):
    B, T, _ = proj.shape
    tq, tk = min(tq, T), min(tk, T)
    assert T % tq == 0 and tq % tk == 0 and N_HEADS % hb == 0
    W = hb * HEAD_DIM
    ng = N_HEADS // hb
    first = COL_SB // hb
    seq = lambda off: pl.BlockSpec((None, T, W), lambda b, g, i: (b, 0, off + g))
    return pl.pallas_call(
        functools.partial(_sb_kernel, tk=tk, hb=hb), grid=(B, ng, T // tq),
        in_specs=[pl.BlockSpec((None, tq, W), lambda b, g, i: (b, i, first + g)),
                  seq(first + ng), seq(first + 2 * ng)],
        out_specs=pl.BlockSpec((None, tq, W), lambda b, g, i: (b, i, g)),
        out_shape=jax.ShapeDtypeStruct((B, T, GROUP_WIDTH), BF16),
        compiler_params=_cparams("parallel", "parallel", "parallel"),
    )(proj, proj, proj)


def _head_layout(page_rows):
    cols = page_rows * N_HEADS
    r = lax.broadcasted_iota(jnp.int32, (N_HEADS, cols), 0)
    c = lax.broadcasted_iota(jnp.int32, (N_HEADS, cols), 1)
    own = (c % N_HEADS) == r
    tok = c // N_HEADS
    return own, tok


def _lane_chunks(x):
    return jnp.concatenate([x[:, j * LANE:(j + 1) * LANE] for j in range(x.shape[1] // LANE)], axis=0)


def _later_sums(x, upper):
    n = x.shape[1] // LANE
    within = _dot_x01(_lane_chunks(x), upper)
    out = [None] * n
    tail = jnp.zeros((N_HEADS, 1), F32)
    for j in reversed(range(n)):
        wj = within[j * N_HEADS:(j + 1) * N_HEADS, :]
        out[j] = wj + tail
        tail = tail + wj[:, 0:1] + x[:, j * LANE:j * LANE + 1]
    return jnp.concatenate(out, axis=1), tail


def _page_rows(kv_refs, which):
    P = kv_refs[0].shape[0]
    return jnp.concatenate([r[:, which, :, :].reshape(P * N_HEADS, HEAD_DIM) for r in kv_refs], axis=0)


def _sb_step_kernel(pt_ref, q_ref, *refs, g):
    kv_refs, (o_ref, run_ref, acc_ref) = refs[:g][::-1], refs[g:]
    s = pl.program_id(1)
    P = kv_refs[0].shape[0]

    @pl.when(s == 0)
    def _():
        run_ref[...] = jnp.zeros_like(run_ref)
        acc_ref[...] = jnp.zeros_like(acc_ref)

    own, _ = _head_layout(g * P)
    r = lax.broadcasted_iota(jnp.int32, (LANE, LANE), 0)
    c = lax.broadcasted_iota(jnp.int32, (LANE, LANE), 1)
    upper = jnp.where(r > c, 1.0, 0.0).astype(BF16)
    z = _dot_nt(q_ref[...], _page_rows(kv_refs, 0)) * ATT_SCALE
    sp = _softplus(z)
    log_keep = jnp.where(own, -sp, 0.0)
    later, total = _later_sums(log_keep, upper)
    run = run_ref[:, 0:1]
    w = jnp.where(own, jnp.exp(z - sp + later + run), 0.0)
    acc_ref[...] += _dot(w, _page_rows(kv_refs, 1))
    run_ref[...] = jnp.broadcast_to(run + total, run_ref.shape)

    @pl.when(s == pl.num_programs(1) - 1)
    def _():
        o_ref[...] = acc_ref[...]


STEP_PAGES = 4


def _sb_step(q, cache, page_table, layer):
    B = q.shape[0]
    n_pages = page_table.shape[1]
    P = cache.shape[2]
    g = STEP_PAGES
    assert n_pages % g == 0
    pt = jnp.concatenate([page_table.reshape(-1), jnp.reshape(layer, (1,)).astype(jnp.int32)])

    def page(t):
        return pl.BlockSpec((None, None, P, 2, N_HEADS, HEAD_DIM),
                            lambda b, s, pt: (pt[B * n_pages], pt[b * n_pages + n_pages - 1 - (s * g + t)], 0, 0, 0, 0))

    return pl.pallas_call(
        functools.partial(_sb_step_kernel, g=g),
        grid_spec=pltpu.PrefetchScalarGridSpec(
            num_scalar_prefetch=1, grid=(B, n_pages // g),
            in_specs=[pl.BlockSpec((None, N_HEADS, HEAD_DIM), lambda b, s, pt: (b, 0, 0))] + [page(t) for t in range(g)],
            out_specs=pl.BlockSpec((None, N_HEADS, HEAD_DIM), lambda b, s, pt: (b, 0, 0)),
            scratch_shapes=[pltpu.VMEM((N_HEADS, LANE), F32), pltpu.VMEM((N_HEADS, HEAD_DIM), F32)]),
        out_shape=jax.ShapeDtypeStruct((B, N_HEADS, HEAD_DIM), F32),
        compiler_params=_cparams("parallel", "arbitrary"),
    )(pt, q, *([cache] * g))


def _dot_nt_precise(a, b):
    ah, al = _split(a)
    bh, bl = _split(b)
    return _dot_nt(ah, bh) + _dot_nt(ah, bl) + _dot_nt(al, bh)


def _topk_mask(gate, n_valid, k):
    n = gate.shape[0]
    idx = lax.broadcasted_iota(jnp.int32, gate.shape, 0)
    rank = jnp.zeros(gate.shape, F32)
    for j in range(n):
        gj = gate[j:j + 1, :]
        ahead = (gj > gate) | ((gj == gate) & (j < idx))
        rank = rank + jnp.where(ahead, jnp.where(j < n_valid, 1.0, 0.0), 0.0)
    return jnp.where(idx < n_valid, jnp.where(rank < k, 1.0, 0.0), 0.0)


def _moba_kernel(slope_ref, q_ref, k_ref, v_ref, o_ref, sel_ref, *, hb):
    g, i = pl.program_id(1), pl.program_id(2)
    bs = q_ref.shape[0]
    nb = k_ref.shape[0] // bs
    lanes = [slice(h * HEAD_DIM, (h + 1) * HEAD_DIM) for h in range(hb)]
    slope = [slope_ref[g * hb + h] for h in range(hb)]
    q = [q_ref[:, sl] for sl in lanes]
    r = lax.broadcasted_iota(jnp.int32, (bs, bs), 0)
    c = lax.broadcasted_iota(jnp.int32, (bs, bs), 1)
    eye = jnp.where(r == c, 1.0, 0.0)
    for h, sl in enumerate(lanes):
        kmean = jnp.concatenate(
            [jnp.mean(k_ref[j * bs:(j + 1) * bs, sl], axis=0, keepdims=True) for j in range(nb)], axis=0)
        gate = _dot_nt_precise(kmean, q[h])
        sel = _dot_nt(eye, _topk_mask(gate, i, MOBA_TOPK))
        for j in range(nb):
            sel_ref[h, j] = sel[:, j:j + 1]

    rel = (r - c).astype(F32)
    own = pl.ds(pl.multiple_of(i * bs, bs), bs)
    s = [_dot_nt(qi, k_ref[own, sl]) * ATT_SCALE - sp * rel for qi, sl, sp in zip(q, lanes, slope)]
    s = [jnp.where(c <= r, si, NEG) for si in s]
    m = [jnp.max(si, axis=-1, keepdims=True) for si in s]
    p = [jnp.exp(si - mi) for si, mi in zip(s, m)]
    l = [jnp.sum(pi, axis=-1, keepdims=True) for pi in p]
    acc = [_dot(pi, v_ref[own, sl]) for pi, sl in zip(p, lanes)]

    def body(j, carry):
        m, l, acc = carry
        rows = pl.ds(pl.multiple_of(j * bs, bs), bs)
        dist = rel + ((i - j) * bs).astype(F32)
        s = [_dot_nt(qi, k_ref[rows, sl]) * ATT_SCALE - sp * dist for qi, sl, sp in zip(q, lanes, slope)]
        s = [jnp.where(sel_ref[h, j] > 0.0, si, NEG) for h, si in enumerate(s)]
        m_new = [jnp.maximum(mi, jnp.max(si, axis=-1, keepdims=True)) for mi, si in zip(m, s)]
        a = [jnp.exp(mi - ni) for mi, ni in zip(m, m_new)]
        p = [jnp.exp(si - ni) for si, ni in zip(s, m_new)]
        l = [ai * li + jnp.sum(pi, axis=-1, keepdims=True) for ai, li, pi in zip(a, l, p)]
        acc = [ai * ci + _dot(pi, v_ref[rows, sl]) for ai, ci, pi, sl in zip(a, acc, p, lanes)]
        return tuple(m_new), tuple(l), tuple(acc)

    m, l, acc = lax.fori_loop(0, i, body, (tuple(m), tuple(l), tuple(acc)))
    for ai, li, sl in zip(acc, l, lanes):
        o_ref[:, sl] = (ai / li).astype(o_ref.dtype)


def _moba_attention(proj, slopes, *, hb=4):
    B, T, _ = proj.shape
    bs = min(MOBA_BLOCK, T)
    assert T % bs == 0 and N_HEADS % hb == 0
    nb = T // bs
    W = hb * HEAD_DIM
    ng = N_HEADS // hb
    first = COL_MB // hb
    seq = lambda off: pl.BlockSpec((None, T, W), lambda b, g, i, sl: (b, 0, off + g))
    return pl.pallas_call(
        functools.partial(_moba_kernel, hb=hb),
        grid_spec=pltpu.PrefetchScalarGridSpec(
            num_scalar_prefetch=1, grid=(B, ng, nb),
            in_specs=[pl.BlockSpec((None, bs, W), lambda b, g, i, sl: (b, i, first + g)),
                      seq(first + ng), seq(first + 2 * ng)],
            out_specs=pl.BlockSpec((None, bs, W), lambda b, g, i, sl: (b, i, g)),
            scratch_shapes=[pltpu.VMEM((hb, nb, bs, 1), F32)]),
        out_shape=jax.ShapeDtypeStruct((B, T, GROUP_WIDTH), BF16),
        compiler_params=_cparams("parallel", "parallel", "parallel"),
    )(slopes, proj, proj, proj)


def _moba_step_kernel(pt_ref, q_ref, ks_ref, vs_ref, slope_ref, *refs, g, bps):
    kv_all, (o_ref, m_ref, l_ref, acc_ref, g_ref) = refs[:g * bps], refs[g * bps:]
    s = pl.program_id(1)
    nb = pl.num_programs(1) * bps
    P = kv_all[0].shape[0]
    own, tok = _head_layout(g * P)
    q = q_ref[...]
    slope = slope_ref[...]
    for t in range(bps):
        kv_refs = kv_all[t * g:(t + 1) * g]
        blk = s * bps + t
        k2 = _page_rows(kv_refs, 0)
        dist = ((nb - blk) * (g * P)).astype(F32) - tok.astype(F32)
        z = _dot_nt(q, k2) * ATT_SCALE - slope * dist
        z = jnp.where(own, z, NEG)
        m = jnp.max(z, axis=-1, keepdims=True)
        p = jnp.where(own, jnp.exp(z - m), 0.0)
        m_ref[blk] = jnp.broadcast_to(m, (N_HEADS, LANE))
        l_ref[blk] = jnp.broadcast_to(jnp.sum(p, axis=-1, keepdims=True), (N_HEADS, LANE))
        acc_ref[blk] = _dot(p, _page_rows(kv_refs, 1))
        kmean = jnp.sum(k2.reshape(g * P, N_HEADS, HEAD_DIM), axis=0) * (1.0 / MOBA_BLOCK)
        gq, gk = q.astype(BF16).astype(F32), kmean.astype(BF16).astype(F32)
        g_ref[blk] = jnp.broadcast_to(jnp.sum(gq * gk, axis=-1, keepdims=True), (N_HEADS, LANE))

    @pl.when(s == pl.num_programs(1) - 1)
    def _():
        n = g_ref.shape[0]
        gate = [g_ref[j] for j in range(n)]
        sel = []
        for j in range(n):
            rank = jnp.zeros((N_HEADS, LANE), F32)
            for t in range(n):
                ahead = (gate[t] > gate[j]) if t > j else (gate[t] >= gate[j])
                if t != j:
                    rank = rank + jnp.where(ahead, 1.0, 0.0)
            sel.append(rank < MOBA_TOPK)
        s_self = jnp.sum(q * ks_ref[...], axis=-1, keepdims=True) * ATT_SCALE
        mx = jnp.broadcast_to(s_self, (N_HEADS, LANE))
        for j in range(n):
            mx = jnp.where(sel[j], jnp.maximum(mx, m_ref[j]), mx)
        e_self = jnp.exp(s_self - mx)
        den = e_self
        num = e_self * vs_ref[...]
        for j in range(n):
            wgt = jnp.where(sel[j], jnp.exp(m_ref[j] - mx), 0.0)
            den = den + wgt * l_ref[j]
            num = num + wgt * acc_ref[j]
        o_ref[...] = num / den


def _moba_step(q, k_self, v_self, cache, page_table, layer, slopes):
    B = q.shape[0]
    n_pages = page_table.shape[1]
    P = cache.shape[2]
    assert MOBA_BLOCK % P == 0 and (n_pages * P) % MOBA_BLOCK == 0
    g = MOBA_BLOCK // P
    nb = n_pages // g
    bps = max(1, STEP_PAGES // g)
    assert nb % bps == 0
    gp = g * bps
    pt = jnp.concatenate([page_table.reshape(-1), jnp.reshape(layer, (1,)).astype(jnp.int32)])
    head = pl.BlockSpec((None, N_HEADS, HEAD_DIM), lambda b, s, pt: (b, 0, 0))
    per_block = pltpu.VMEM((nb, N_HEADS, LANE), F32)

    def page(t):
        return pl.BlockSpec((None, None, P, 2, N_HEADS, HEAD_DIM),
                            lambda b, s, pt: (pt[B * n_pages], pt[b * n_pages + s * gp + t], 0, 0, 0, 0))

    return pl.pallas_call(
        functools.partial(_moba_step_kernel, g=g, bps=bps),
        grid_spec=pltpu.PrefetchScalarGridSpec(
            num_scalar_prefetch=1, grid=(B, nb // bps),
            in_specs=[head, head, head, pl.BlockSpec((N_HEADS, 1), lambda b, s, pt: (0, 0))] + [page(t) for t in range(gp)],
            out_specs=head,
            scratch_shapes=[per_block, per_block, per_block, per_block]),
        out_shape=jax.ShapeDtypeStruct((B, N_HEADS, HEAD_DIM), F32),
        compiler_params=_cparams("parallel", "arbitrary"),
    )(pt, q, k_self, v_self, slopes.reshape(N_HEADS, 1), *([cache] * gp))


def _l2n(x):
    return x * lax.rsqrt(jnp.sum(x * x, axis=-1, keepdims=True) + EPS)


def _dn_prep_kernel(x_ref, tail_ref, st_ref, w_ref, b_ref, o_ref):
    part = pl.program_id(2)
    first = pl.program_id(1) == 0
    K = w_ref.shape[0]
    y = _silu(_conv_rows(x_ref[...], _prev_rows(first, st_ref, tail_ref, K - 1), w_ref, b_ref))

    @pl.when(part == 2)
    def _():
        o_ref[...] = y

    @pl.when(part < 2)
    def _():
        scale = jnp.where(part == 0, HEAD_DIM ** -0.5, 1.0)
        for hh in range(N_HEADS):
            sl = slice(hh * HEAD_DIM, (hh + 1) * HEAD_DIM)
            o_ref[:, sl] = _l2n(y[:, sl]) * scale


def _dn_prep(proj, prev, conv_w, conv_b, *, tt=256):
    B, T, _ = proj.shape
    K = conv_w.shape[0]
    GW = GROUP_WIDTH
    tt = min(tt, T)
    assert T % tt == 0 and tt % SUBLANE == 0
    tb = tt // SUBLANE
    cw = COL_DN * LANE // GW
    return pl.pallas_call(
        _dn_prep_kernel, grid=(B, T // tt, 3),
        in_specs=[pl.BlockSpec((None, tt, GW), lambda b, i, p: (b, i, cw + p)),
                  pl.BlockSpec((None, SUBLANE, GW), lambda b, i, p: (b, jnp.maximum(i * tb - 1, 0), cw + p)),
                  pl.BlockSpec((None, K - 1, GW), lambda b, i, p: (b, 0, p)),
                  pl.BlockSpec((K, GW), lambda b, i, p: (0, p)),
                  pl.BlockSpec((1, GW), lambda b, i, p: (0, p))],
        out_specs=pl.BlockSpec((None, tt, GW), lambda b, i, p: (b, i, p)),
        out_shape=jax.ShapeDtypeStruct((B, T, 3 * GW), F32),
        compiler_params=_cparams("parallel", "parallel", "parallel"),
    )(proj, proj, prev, conv_w, conv_b.reshape(1, 3 * GW))


def _dn_gate_values(pe, alog_ref, dtb_ref):
    col = lax.broadcasted_iota(jnp.int32, pe.shape, 1)
    g = -jnp.exp(alog_ref[...]) * _softplus(pe + dtb_ref[...])
    return col, jnp.where(col < N_HEADS, jax.nn.sigmoid(pe), g)


def _dn_gates_kernel(pe_ref, alog_ref, dtb_ref, o_ref):
    col, val = _dn_gate_values(pe_ref[...], alog_ref, dtb_ref)
    rows = lax.broadcasted_iota(jnp.int32, val.shape, 0)
    dec = jnp.where(col < N_HEADS, 0.0, val)
    d = 1
    while d < min(DN_CHUNK, val.shape[0]):
        dec = dec + jnp.where((rows % DN_CHUNK) >= d, pltpu.roll(dec, d, 0), 0.0)
        d *= 2
    o_ref[...] = jnp.where(col < N_HEADS, val, dec)


def _gate_rows(a_log, dt_bias):
    pad = lambda a: jnp.zeros((1, LANE), F32).at[0, N_HEADS:2 * N_HEADS].set(a)
    return pad(a_log), pad(dt_bias)


def _dn_gates(pe, a_log, dt_bias, *, tt=512):
    B, T, _ = pe.shape
    tt = min(tt, T)
    assert T % tt == 0 and (tt % DN_CHUNK == 0 or tt == T)
    alog, dtb = _gate_rows(a_log, dt_bias)
    row = pl.BlockSpec((1, LANE), lambda b, i: (0, 0))
    blk = pl.BlockSpec((None, tt, LANE), lambda b, i: (b, i, 0))
    return pl.pallas_call(
        _dn_gates_kernel, grid=(B, T // tt), in_specs=[blk, row, row], out_specs=blk,
        out_shape=jax.ShapeDtypeStruct((B, T, LANE), F32),
        compiler_params=_cparams("parallel", "parallel"),
    )(pe, alog, dtb)


def _pick_col(x, j):
    col = lax.broadcasted_iota(jnp.int32, x.shape, 1)
    return jnp.sum(jnp.where(col == j, x, 0.0), axis=-1, keepdims=True)


def _dot_precise(a, b):
    ah, al = _split(a)
    bh, bl = _split(b)
    return (jnp.dot(ah, bh, preferred_element_type=F32) + jnp.dot(ah, bl, preferred_element_type=F32)
            + jnp.dot(al, bh, preferred_element_type=F32))


SOLVE_BLOCK = 16
DN_SUPER = 4


def _unit_lower_solve(a, rhs, span):
    n = a[0].shape[0]
    r = lax.broadcasted_iota(jnp.int32, (n, n), 0)
    c = lax.broadcasted_iota(jnp.int32, (n, n), 1)
    in_diag = (r // SOLVE_BLOCK) == (c // SOLVE_BLOCK)
    eye = jnp.where(r == c, 1.0, 0.0)
    a_off = [jnp.where(in_diag, 0.0, ai) for ai in a]
    p = [jnp.where(in_diag, -ai, 0.0) for ai in a]
    t = [eye + pi for pi in p]
    m = 2
    while m < min(SOLVE_BLOCK, span):
        p = [_dot_precise(pi, pi) for pi in p]
        t = [ti + _dot_precise(pi, ti) for pi, ti in zip(p, t)]
        m *= 2
    x = [_dot_precise(ti, ri) for ti, ri in zip(t, rhs)]
    for _ in range(-(-span // SOLVE_BLOCK) - 1):
        y = [_dot_precise(ai, xi) for ai, xi in zip(a_off, x)]
        x = [_dot_precise(ti, ri - yi) for ti, ri, yi in zip(t, rhs, y)]
    return x


def _dn_chunks(q, k, v, beta, dcol, drow, S, C):
    H = len(q)
    R = q[0].shape[0]
    r = lax.broadcasted_iota(jnp.int32, (R, R), 0)
    c = lax.broadcasted_iota(jnp.int32, (R, R), 1)
    low = jnp.where((r // C) == (c // C), r - c, -1)
    lmat = [jnp.where(low >= 0, jnp.exp(jnp.where(low >= 0, dc - dr, 0.0)), 0.0) for dc, dr in zip(dcol, drow)]
    kb = [ki * bi for ki, bi in zip(k, beta)]
    a = [jnp.where(low > 0, _dot_nt(kbi, ki) * li, 0.0) for kbi, ki, li in zip(kb, k, lmat)]
    rhs = [jnp.concatenate([vi * bi, kbi * jnp.exp(dc)], axis=1) for vi, bi, kbi, dc in zip(v, beta, kb, dcol)]
    x = _unit_lower_solve(a, rhs, C)
    u = [xi[:, :HEAD_DIM] for xi in x]
    w = [xi[:, HEAD_DIM:] for xi in x]
    qk = [_dot_nt(qi, ki) * li for qi, ki, li in zip(q, k, lmat)]
    qe = [qi * jnp.exp(dc) for qi, dc in zip(q, dcol)]
    spans = [(j * C, (j + 1) * C) for j in range(R // C)]
    S = list(S)
    s_in = [[] for _ in range(H)]
    for lo, hi in spans:
        for h in range(H):
            d_last = dcol[h][hi - 1:hi, :]
            kdt = (k[h][lo:hi] * jnp.exp(d_last - dcol[h][lo:hi])).T
            s_in[h].append(S[h])
            S[h] = S[h] * jnp.exp(d_last) - _dot(_dot(kdt, w[h][lo:hi]), S[h]) + _dot(kdt, u[h][lo:hi])
    outs = []
    for h in range(H):
        v_new = jnp.concatenate([u[h][lo:hi] - _dot(w[h][lo:hi], s) for (lo, hi), s in zip(spans, s_in[h])], axis=0)
        outs.append(jnp.concatenate([_dot(qe[h][lo:hi], s) for (lo, hi), s in zip(spans, s_in[h])], axis=0)
                    + _dot(qk[h], v_new))
    return outs, S


def _dn_kernel(q_ref, k_ref, v_ref, gate_ref, bd_ref, bdt_ref, s0_ref, ng_ref, o_ref, s_ref, *, hb, chunk):
    hg = pl.program_id(1)
    T = q_ref.shape[0]
    R = bdt_ref.shape[2]
    lanes = [slice(hh * HEAD_DIM, (hh + 1) * HEAD_DIM) for hh in range(hb)]

    def body(si, states):
        r0 = pl.multiple_of(si * R, R)
        bd = bd_ref[pl.ds(r0, R), :]
        rows = pl.ds(r0, R)
        outs, new = _dn_chunks([q_ref[rows, sl] for sl in lanes], [k_ref[rows, sl] for sl in lanes],
                               [v_ref[rows, sl] for sl in lanes],
                               [_pick_col(bd, hg * hb + hh) for hh in range(hb)],
                               [_pick_col(bd, N_HEADS + hg * hb + hh) for hh in range(hb)],
                               [bdt_ref[hh, pl.ds(si, 1), :] for hh in range(hb)], states, chunk)
        for o, sl in zip(outs, lanes):
            y = _rms(o, ng_ref[...]) * _silu(gate_ref[rows, sl])
            o_ref[rows, sl] = y.astype(o_ref.dtype)
        return tuple(new)

    states = lax.fori_loop(0, T // R, body, tuple(s0_ref[hh] for hh in range(hb)))
    for hh in range(hb):
        s_ref[hh] = states[hh]


def _dn_attention(qkv, proj, bd, S0, norm_g, *, hb=4):
    B, T, _ = qkv.shape
    C = min(DN_CHUNK, T)
    R = min(DN_SUPER * C, T)
    assert T % R == 0 and R % C == 0
    W = hb * HEAD_DIM
    ng = N_HEADS // hb
    bdt = jnp.swapaxes(bd[:, :, N_HEADS:2 * N_HEADS], 1, 2).reshape(B, N_HEADS, T // R, R)
    seq = lambda off: pl.BlockSpec((None, T, W), lambda b, g: (b, 0, off + g))
    st = pl.BlockSpec((None, hb, HEAD_DIM, HEAD_DIM), lambda b, g: (b, g, 0, 0))
    return pl.pallas_call(
        functools.partial(_dn_kernel, hb=hb, chunk=C), grid=(B, ng),
        in_specs=[seq(0), seq(ng), seq(2 * ng),
                  pl.BlockSpec((None, T, W), lambda b, g: (b, 0, COL_DG * LANE // W + g)),
                  pl.BlockSpec((None, T, LANE), lambda b, g: (b, 0, 0)),
                  pl.BlockSpec((None, hb, T // R, R), lambda b, g: (b, g, 0, 0)),
                  st, pl.BlockSpec((1, HEAD_DIM), lambda b, g: (0, 0))],
        out_specs=(pl.BlockSpec((None, T, W), lambda b, g: (b, 0, g)), st),
        out_shape=(jax.ShapeDtypeStruct((B, T, GROUP_WIDTH), BF16),
                   jax.ShapeDtypeStruct((B, N_HEADS, HEAD_DIM, HEAD_DIM), F32)),
        compiler_params=_cparams("parallel", "parallel"),
    )(qkv, qkv, qkv, proj, bd, bdt, S0, norm_g.reshape(1, HEAD_DIM))


def _dn_step_prep_kernel(x_ref, prev_ref, w_ref, b_ref, pe_ref, alog_ref, dtb_ref, o_ref, bg_ref):
    K = w_ref.shape[0]
    y = b_ref[...] + x_ref[...] * w_ref[K - 1:K, :]
    for k in range(K - 1):
        y = y + prev_ref[:, k, :] * w_ref[k:k + 1, :]
    y = _silu(y)
    for part in range(3):
        for hh in range(N_HEADS):
            lo = part * GROUP_WIDTH + hh * HEAD_DIM
            seg = y[:, lo:lo + HEAD_DIM]
            if part == 0:
                seg = _l2n(seg) * HEAD_DIM ** -0.5
            elif part == 1:
                seg = _l2n(seg)
            o_ref[:, lo:lo + HEAD_DIM] = seg
    _, bg_ref[...] = _dn_gate_values(pe_ref[...], alog_ref, dtb_ref)


def _dn_step_prep(x, prev, conv_w, conv_b, pe, a_log, dt_bias):
    B, W = x.shape
    alog, dtb = _gate_rows(a_log, dt_bias)
    return pl.pallas_call(
        _dn_step_prep_kernel,
        out_shape=(jax.ShapeDtypeStruct((B, W), F32), jax.ShapeDtypeStruct((B, LANE), F32)),
    )(x, prev, conv_w, conv_b.reshape(1, W), pe, alog, dtb)


def _dn_step_kernel(q_ref, k_ref, v_ref, gate_ref, bg_ref, s0_ref, ng_ref, o_ref, s_ref):
    h = pl.program_id(1)
    q, k, v = q_ref[...], k_ref[...], v_ref[...]
    bg = bg_ref[...]
    beta = _pick_col(bg, h)
    e = jnp.exp(_pick_col(bg, N_HEADS + h))
    S = s0_ref[...]
    rows = jnp.concatenate([k * beta * e, q * e, jnp.zeros((SUBLANE - 2, HEAD_DIM), F32)], axis=0)
    rs = _dot(rows, S)
    v_new = v * beta - rs[0:1, :]
    o = rs[1:2, :] + jnp.sum(q * k, axis=-1, keepdims=True) * v_new
    r = lax.broadcasted_iota(jnp.int32, (HEAD_DIM, HEAD_DIM), 0)
    c = lax.broadcasted_iota(jnp.int32, (HEAD_DIM, HEAD_DIM), 1)
    kcol = jnp.sum(jnp.where(r == c, jnp.broadcast_to(k, (HEAD_DIM, HEAD_DIM)), 0.0), axis=-1, keepdims=True)
    s_ref[...] = S * e + kcol * v_new
    o_ref[...] = _rms(o, ng_ref[...]) * _silu(gate_ref[...])


def _dn_step(qkv, gate, bg, S0, norm_g):
    B = qkv.shape[0]
    head = lambda off: pl.BlockSpec((None, 1, HEAD_DIM), lambda b, h: (b, 0, off + h))
    st = pl.BlockSpec((None, None, HEAD_DIM, HEAD_DIM), lambda b, h: (b, h, 0, 0))
    return pl.pallas_call(
        _dn_step_kernel, grid=(B, N_HEADS),
        in_specs=[head(0), head(N_HEADS), head(2 * N_HEADS), head(0),
                  pl.BlockSpec((None, 1, LANE), lambda b, h: (b, 0, 0)), st,
                  pl.BlockSpec((1, HEAD_DIM), lambda b, h: (0, 0))],
        out_specs=(head(0), st),
        out_shape=(jax.ShapeDtypeStruct((B, 1, GROUP_WIDTH), F32),
                   jax.ShapeDtypeStruct((B, N_HEADS, HEAD_DIM, HEAD_DIM), F32)),
        compiler_params=_cparams("parallel", "parallel"),
    )(qkv, qkv, qkv, gate, bg, S0, norm_g.reshape(1, HEAD_DIM))


def _project_in(h2, w, layer, *, tn=1024, precise=False):
    proj = _matmul(h2, w['w_in'], layer, tm=1024, tn=tn, n_out=IN_MAIN, precise=precise)
    pe = _matmul(h2, w['w_in_tail'], layer, tm=1024, tn=LANE, precise=precise)
    return proj, pe


def _state_to_rows(s):
    B = s.shape[0]
    return jnp.moveaxis(s, -1, 1).reshape(B, 2, -1)


def _rows_to_state(s):
    B = s.shape[0]
    return jnp.moveaxis(s.reshape(B, 2, S5_GROUPS, S5_STATE), 1, -1)


def _kv_out(proj, col, B, T):
    w = 2 * GROUP_WIDTH
    lo = col * LANE + GROUP_WIDTH
    return proj[:, lo:lo + w].reshape(B, T, 2, N_HEADS, HEAD_DIM)


def _layer(carry, lw, consts):
    xp, xs = carry
    (mod, st_s5, st_dn, st_dn_conv, st_ffn, layer) = lw['dyn']
    W = lw['w']
    cache_sb, cache_mb, page_table, slopes, wb = consts
    Bp, Tp, D = xp.shape
    Bs = xs.shape[0]
    F2 = wb['ffn_w_up'].shape[2]

    s5p = _s5_params(W['s5_lambda_re'], W['s5_lambda_im'], W['s5_log_dt'], W['s5_b_re'], W['s5_b_im'])
    packed_f = _s5_pack(*s5p, W['s5_c_re'], W['s5_c_im'])
    packed = packed_f[:2] + tuple(a.astype(BF16) for a in packed_f[2:])

    def mods(m):
        m = m.reshape(m.shape[0], N_MOD, 1, D)
        return [m[:, i] for i in range(N_MOD)]

    mp, ms = mods(mod[:Bp]), mods(mod[Bp:Bp + Bs])
    row = lambda a: a.reshape(1, -1)

    h = _prenorm(xp, row(W['g_pre_mix']), mp[1], mp[0])
    proj, pe = _project_in(h.reshape(Bp * Tp, D), wb, layer)
    proj3, pe3 = proj.reshape(Bp, Tp, IN_MAIN), pe.reshape(Bp, Tp, LANE)
    y_s5, hs = _s5_scan(proj3, jnp.zeros((Bp, 2, S5_GROUPS * S5_STATE), F32), packed, W['s5_d'])
    y_s5 = _glu(y_s5.reshape(Bp * Tp, GROUP_WIDTH), W['s5_w_glu'].astype(BF16), W['s5_b_glu'])
    y_sb = _sb_attention(proj3)
    y_mb = _moba_attention(proj3, slopes)
    K_dn = W['dn_conv_w'].shape[0]
    qkv_dn = _dn_prep(proj3, jnp.zeros((Bp, K_dn - 1, 3 * GROUP_WIDTH), F32), W['dn_conv_w'], W['dn_conv_b'])
    bd = _dn_gates(pe3, W['dn_a_log'], W['dn_dt_bias'])
    y_dn, dn_S = _dn_attention(qkv_dn, proj3, bd, jnp.zeros((Bp, N_HEADS, HEAD_DIM, HEAD_DIM), F32), W['dn_norm_g'])
    ycat = jnp.concatenate([y_s5.reshape(Bp, Tp, GROUP_WIDTH), y_sb, y_mb, y_dn], axis=-1)
    y = _matmul(ycat.reshape(Bp * Tp, 4 * GROUP_WIDTH), wb['w_out'], layer, tm=1024, tn=1024)
    xp, h = _resid_norm(xp, y.reshape(Bp, Tp, D), mp[2], row(W['g_post_mix']), row(W['g_pre_ffn']), mp[4], mp[3])
    up = _matmul(h.reshape(Bp * Tp, D), wb['ffn_w_up'], layer, tm=2048, tn=512).reshape(Bp, Tp, F2)
    K_f = W['ffn_conv_w'].shape[0]
    act = _ffn_gate(up, jnp.zeros((Bp, K_f - 1, F2), F32), W['ffn_conv_w'], W['ffn_conv_b'])
    y = _matmul(act.reshape(Bp * Tp, F2 // 2), wb['ffn_w_down'], layer, tm=1024, tn=512, tk=F2 // 4)
    xp = _resid(xp, y.reshape(Bp, Tp, D), mp[5], row(W['g_post_ffn']))
    dn_lo = COL_DN * LANE
    out_p = (_kv_out(proj, COL_SB, Bp, Tp), _kv_out(proj, COL_MB, Bp, Tp), _rows_to_state(hs), dn_S,
             proj3[:, Tp - (K_dn - 1):, dn_lo:dn_lo + 3 * GROUP_WIDTH], up[:, Tp - (K_f - 1):, :])

    h = _prenorm(xs, row(W['g_pre_mix']), ms[1], ms[0])
    proj, pe = _project_in(h.reshape(Bs, D), wb, layer)
    y_s5, hs = _s5_step(proj[:, :GROUP_WIDTH], _state_to_rows(st_s5), packed, W['s5_d'])
    y_s5 = _glu(y_s5, W['s5_w_glu'].astype(BF16), W['s5_b_glu'])
    heads = lambda col: proj[:, col * LANE:col * LANE + GROUP_WIDTH].reshape(Bs, N_HEADS, HEAD_DIM)
    y_sb = _sb_step(heads(COL_SB), cache_sb, page_table, layer)
    y_mb = _moba_step(heads(COL_MB), heads(COL_MB + N_HEADS), heads(COL_MB + 2 * N_HEADS),
                      cache_mb, page_table, layer, slopes)
    x_dn = proj[:, dn_lo:dn_lo + 3 * GROUP_WIDTH]
    qkv_dn, bg = _dn_step_prep(x_dn, st_dn_conv, W['dn_conv_w'], W['dn_conv_b'], pe, W['dn_a_log'], W['dn_dt_bias'])
    gate_dn = proj[:, COL_DG * LANE:COL_DG * LANE + GROUP_WIDTH]
    y_dn, dn_S = _dn_step(qkv_dn.reshape(Bs, 1, -1), gate_dn.reshape(Bs, 1, -1), bg.reshape(Bs, 1, LANE),
                          st_dn, W['dn_norm_g'])
    ycat = jnp.concatenate([y_s5, y_sb.reshape(Bs, -1).astype(BF16), y_mb.reshape(Bs, -1).astype(BF16),
                            y_dn.reshape(Bs, -1).astype(BF16)], axis=-1)
    y = _matmul(ycat, wb['w_out'], layer, tm=Bs, tn=1024)
    xs, h = _resid_norm(xs, y.reshape(Bs, 1, D), ms[2], row(W['g_post_mix']), row(W['g_pre_ffn']), ms[4], ms[3])
    up = _matmul(h.reshape(Bs, D), wb['ffn_w_up'], layer, tm=Bs, tn=512)
    act = _ffn_gate_step(up, st_ffn, W['ffn_conv_w'], W['ffn_conv_b'])
    y = _matmul(act, wb['ffn_w_down'], layer, tm=Bs, tn=512, tk=F2 // 4)
    xs = _resid(xs, y.reshape(Bs, 1, D), ms[5], row(W['g_post_ffn']))
    out_s = (_kv_out(proj, COL_SB, Bs, 1), _kv_out(proj, COL_MB, Bs, 1), _rows_to_state(hs), dn_S,
             jnp.concatenate([st_dn_conv[:, 1:], x_dn[:, None, :]], axis=1),
             jnp.concatenate([st_ffn[:, 1:], up[:, None, :]], axis=1))
    return (xp, xs), (out_p, out_s)


def kernel(x_prompt, x_sample, cache_sb_kv, cache_moba_kv, state_s5, state_dn, state_dn_conv, state_ffn_conv, page_table, c_prompt, c_sample, w_in, w_out, w_ada, b_ada, g_pre_mix, g_post_mix, g_pre_ffn, g_post_ffn, s5_lambda_re, s5_lambda_im, s5_log_dt, s5_b_re, s5_b_im, s5_c_re, s5_c_im, s5_d, s5_w_glu, s5_b_glu, dn_conv_w, dn_conv_b, dn_a_log, dn_dt_bias, dn_norm_g, ffn_w_up, ffn_conv_w, ffn_conv_b, ffn_w_down):
    depth = w_in.shape[0]
    Bp, Bs = x_prompt.shape[0], x_sample.shape[0]
    rows = -(-(Bp + Bs) // SUBLANE) * SUBLANE
    c_all = jnp.concatenate([c_prompt, c_sample, jnp.zeros((rows - Bp - Bs, c_prompt.shape[1]), F32)], axis=0)
    mod = _ada_all(c_all, w_ada, b_ada)
    slopes = 2.0 ** (-8.0 * (jnp.arange(N_HEADS, dtype=F32) + 1.0) / N_HEADS)
    tail = w_in[:, :, IN_MAIN:]
    wb = dict(w_in=w_in.astype(BF16), w_out=w_out.astype(BF16), ffn_w_up=ffn_w_up.astype(BF16),
              ffn_w_down=ffn_w_down.astype(BF16),
              w_in_tail=jnp.pad(tail, ((0, 0), (0, 0), (0, LANE - tail.shape[2]))).astype(BF16))
    weights = dict(g_pre_mix=g_pre_mix, g_post_mix=g_post_mix, g_pre_ffn=g_pre_ffn,
                   g_post_ffn=g_post_ffn, s5_lambda_re=s5_lambda_re, s5_lambda_im=s5_lambda_im,
                   s5_log_dt=s5_log_dt, s5_b_re=s5_b_re, s5_b_im=s5_b_im, s5_c_re=s5_c_re, s5_c_im=s5_c_im,
                   s5_d=s5_d, s5_w_glu=s5_w_glu, s5_b_glu=s5_b_glu, dn_conv_w=dn_conv_w, dn_conv_b=dn_conv_b,
                   dn_a_log=dn_a_log, dn_dt_bias=dn_dt_bias, dn_norm_g=dn_norm_g,
                   ffn_conv_w=ffn_conv_w, ffn_conv_b=ffn_conv_b)
    dyn = (mod, state_s5, state_dn, state_dn_conv, state_ffn_conv, jnp.arange(depth, dtype=jnp.int32))
    consts = (cache_sb_kv, cache_moba_kv, page_table, slopes, wb)
    step = lambda carry, lw: _layer(carry, lw, consts)
    (xp, xs), (out_p, out_s) = lax.scan(step, (x_prompt, x_sample), dict(dyn=dyn, w=weights))
    outs = [xp, xs]
    for p, s in zip(out_p, out_s):
        outs += [p, s]
    return tuple(outs)
```
